```python
import jax, jax.numpy as jnp
from jax import lax
import numpy as np

D_MODEL = 2048
BATCH = 1
SEQ = 16384
DEPTH = 1

HEAD_DIM = 128
N_HEADS_A = D_MODEL // 256
N_KV_A = max(1, N_HEADS_A // 4)
N_HEADS_B = D_MODEL // 256
GRID_W = 64
NA_ROWS = 8
NA_COLS = 16
Q_BLOCK = 128
ROPE_THETA = 10000.0
ROPE_AXIS_DIM = HEAD_DIM // 2
D_FF = -(-(8 * D_MODEL) // (3 * 256)) * 256
NORM_EPS = 1e-6
IN_SPLITS = (
    N_HEADS_A * HEAD_DIM,
    N_KV_A * HEAD_DIM,
    N_KV_A * HEAD_DIM,
    N_HEADS_B * HEAD_DIM,
    N_HEADS_B * HEAD_DIM,
    N_HEADS_B * HEAD_DIM,
    D_MODEL,
    D_MODEL,
)
D_IN = sum(IN_SPLITS)

kernel_name = "hybrid_gqa_natten_gated_encoder_block"


def rms_norm(x, w):
    xf = x.astype(jnp.float32)
    y = xf * lax.rsqrt(jnp.mean(xf * xf, axis=-1, keepdims=True) + NORM_EPS)
    return (y * w.astype(jnp.float32)).astype(x.dtype)


def axial_rope_tables(S):
    t = jnp.arange(S, dtype=jnp.int32)
    row = (t // GRID_W).astype(jnp.float32)
    col = (t % GRID_W).astype(jnp.float32)
    inv = ROPE_THETA ** (-jnp.arange(0, ROPE_AXIS_DIM, 2, dtype=jnp.float32) / ROPE_AXIS_DIM)
    ang = jnp.concatenate([row[:, None] * inv[None], col[:, None] * inv[None]], axis=-1)
    return jnp.cos(ang), jnp.sin(ang)


def apply_rope(x, cos, sin):
    xf = x.astype(jnp.float32).reshape(*x.shape[:-1], HEAD_DIM // 2, 2)
    x0, x1 = xf[..., 0], xf[..., 1]
    c = cos[None, :, None, :]
    s = sin[None, :, None, :]
    out = jnp.stack([x0 * c - x1 * s, x0 * s + x1 * c], axis=-1)
    return out.reshape(x.shape).astype(x.dtype)


def global_gqa(q, k, v):
    B, S, HQ, hd = q.shape
    KVH = k.shape[2]
    G = HQ // KVH
    nb = S // Q_BLOCK
    scale = hd ** -0.5
    qb = q.reshape(B, nb, Q_BLOCK, KVH, G, hd).transpose(1, 0, 2, 3, 4, 5)

    def one_block(qblk):
        s = jnp.einsum('bqkgd,bskd->bkgqs', qblk, k).astype(jnp.float32) * scale
        p = jax.nn.softmax(s, axis=-1).astype(v.dtype)
        return jnp.einsum('bkgqs,bskd->bqkgd', p, v)

    o = lax.map(one_block, qb)
    return o.transpose(1, 0, 2, 3, 4, 5).reshape(B, S, HQ * hd)


def neighbourhood_attention(q, k, v, rpb):
    B, S, H, hd = q.shape
    rows = S // GRID_W
    kr = min(NA_ROWS, rows)
    nk = kr * NA_COLS
    nb = S // Q_BLOCK
    scale = hd ** -0.5
    t = jnp.arange(S, dtype=jnp.int32)
    r = t // GRID_W
    col = t % GRID_W
    rs = jnp.clip(r - kr // 2, 0, rows - kr)
    cs = jnp.clip(col - NA_COLS // 2, 0, GRID_W - NA_COLS)
    key_r = rs[:, None, None] + jnp.arange(kr, dtype=jnp.int32)[None, :, None]
    key_c = cs[:, None, None] + jnp.arange(NA_COLS, dtype=jnp.int32)[None, None, :]
    shape3 = (S, kr, NA_COLS)
    idx = (key_r * GRID_W + key_c).reshape(nb, Q_BLOCK, nk)
    rel_r = jnp.broadcast_to(key_r - r[:, None, None] + (NA_ROWS - 1), shape3).reshape(nb, Q_BLOCK, nk)
    rel_c = jnp.broadcast_to(key_c - col[:, None, None] + (NA_COLS - 1), shape3).reshape(nb, Q_BLOCK, nk)
    qb = q.reshape(B, nb, Q_BLOCK, H, hd).transpose(1, 0, 2, 3, 4)

    def one_block(args):
        qblk, ib, rr, rc = args
        kb = k[:, ib]
        vb = v[:, ib]
        bias = rpb[:, rr, rc].astype(jnp.float32)
        s = jnp.einsum('bqhd,bqnhd->bhqn', qblk, kb).astype(jnp.float32) * scale + bias[None]
        p = jax.nn.softmax(s, axis=-1).astype(v.dtype)
        return jnp.einsum('bhqn,bqnhd->bqhd', p, vb)

    o = lax.map(one_block, (qb, idx, rel_r, rel_c))
    return o.transpose(1, 0, 2, 3, 4).reshape(B, S, H * hd)


def setup_inputs(seed: int = 0) -> dict:
    key = jax.random.key(seed)
    ks = jax.random.split(key, 20)
    f32 = jnp.float32
    D = D_MODEL

    def w(k, shape, fan_in):
        return jax.random.normal(k, shape, f32) * (fan_in ** -0.5)

    def gain(k, shape):
        return 1.0 + 0.05 * jax.random.normal(k, shape, f32)

    return {
        "x": jax.random.normal(ks[0], (BATCH, SEQ, D), f32),
        "c": jax.random.normal(ks[1], (BATCH, D), f32),
        "w_ada": w(ks[2], (DEPTH, D, 6 * D), D) * 0.5,
        "b_ada": 0.02 * jax.random.normal(ks[3], (DEPTH, 6 * D), f32),
        "norm1_w": gain(ks[4], (DEPTH, D)),
        "w_in": w(ks[5], (DEPTH, D, D_IN), D),
        "q_norm_w": gain(ks[6], (DEPTH, HEAD_DIM)),
        "k_norm_w": gain(ks[7], (DEPTH, HEAD_DIM)),
        "nat_rpb": 0.1 * jax.random.normal(ks[8], (DEPTH, N_HEADS_B, 2 * NA_ROWS - 1, 2 * NA_COLS - 1), f32),
        "w_oa": w(ks[9], (DEPTH, N_HEADS_A * HEAD_DIM, D), N_HEADS_A * HEAD_DIM),
        "w_ob": w(ks[10], (DEPTH, N_HEADS_B * HEAD_DIM, D), N_HEADS_B * HEAD_DIM),
        "w_out": w(ks[11], (DEPTH, D, D), D),
        "norm2_w": gain(ks[12], (DEPTH, D)),
        "w_ffn_gate": w(ks[13], (DEPTH, D, D_FF), D),
        "w_ffn_up": w(ks[14], (DEPTH, D, D_FF), D),
        "w_ffn_down": w(ks[15], (DEPTH, D_FF, D), D_FF),
        "final_w": gain(ks[16], (D,)),
    }


def reference(x, c, w_ada, b_ada, norm1_w, w_in, q_norm_w, k_norm_w, nat_rpb,
              w_oa, w_ob, w_out, norm2_w, w_ffn_gate, w_ffn_up, w_ffn_down, final_w):
    B, S, D = x.shape
    cos, sin = axial_rope_tables(S)
    offs = np.cumsum(IN_SPLITS)[:-1].tolist()
    c_act = jax.nn.silu(c)
    for l in range(DEPTH):
        mod = c_act @ w_ada[l] + b_ada[l]
        sh1, sc1, g1, sh2, sc2, g2 = [m[:, None, :] for m in jnp.split(mod, 6, axis=-1)]

        h = rms_norm(x, norm1_w[l]) * (1.0 + sc1) + sh1
        proj = h @ w_in[l]
        qa, ka, va, qb, kb, vb, ga, gb = jnp.split(proj, offs, axis=-1)
        qa = qa.reshape(B, S, N_HEADS_A, HEAD_DIM)
        ka = ka.reshape(B, S, N_KV_A, HEAD_DIM)
        va = va.reshape(B, S, N_KV_A, HEAD_DIM)
        qa = apply_rope(rms_norm(qa, q_norm_w[l]), cos, sin)
        ka = apply_rope(rms_norm(ka, k_norm_w[l]), cos, sin)
        ya = global_gqa(qa, ka, va)
        yb = neighbourhood_attention(
            qb.reshape(B, S, N_HEADS_B, HEAD_DIM),
            kb.reshape(B, S, N_HEADS_B, HEAD_DIM),
            vb.reshape(B, S, N_HEADS_B, HEAD_DIM),
            nat_rpb[l])
        merged = jax.nn.sigmoid(ga) * (ya @ w_oa[l]) + jax.nn.sigmoid(gb) * (yb @ w_ob[l])
        x = x + g1 * (merged @ w_out[l])

        h2 = rms_norm(x, norm2_w[l]) * (1.0 + sc2) + sh2
        ff = (jax.nn.silu(h2 @ w_ffn_gate[l]) * (h2 @ w_ffn_up[l])) @ w_ffn_down[l]
        x = x + g2 * ff
    return rms_norm(x, final_w)
```

```python
import functools
import math

import jax
import jax.numpy as jnp
import numpy as np
from jax import lax
from jax.experimental import pallas as pl
from jax.experimental.pallas import tpu as pltpu

HEAD_DIM = 128
KV_GROUP = 4
GRID_W = 64
NA_ROWS = 8
NA_COLS = 16
ROPE_THETA = 10000.0
NORM_EPS = 1e-6
NEG_BIG = -1e30
LOG2E = 1.4426950408889634

VMEM_LIMIT = 56 * 1024 * 1024

_NT = (((1,), (1,)), ((), ()))


def _cparams(sem):
    return pltpu.CompilerParams(dimension_semantics=sem, vmem_limit_bytes=VMEM_LIMIT)


def _adaln_kernel(c_ref, w_ref, b_ref, o_ref):
    c = c_ref[...]
    act = c * jax.nn.sigmoid(c)
    o_ref[...] = jnp.sum(act * w_ref[...], axis=0, keepdims=True) + b_ref[...]


def _adaln(c_col, w_ada, b_ada, tn=1024):
    d, n = w_ada.shape
    return pl.pallas_call(
        _adaln_kernel,
        grid=(n // tn,),
        in_specs=[pl.BlockSpec((d, 1), lambda j: (0, 0)),
                  pl.BlockSpec((d, tn), lambda j: (0, j)),
                  pl.BlockSpec((1, tn), lambda j: (0, j))],
        out_specs=pl.BlockSpec((1, tn), lambda j: (0, j)),
        out_shape=jax.ShapeDtypeStruct((1, n), jnp.float32),
        compiler_params=_cparams(("arbitrary",)),
        name="adaln",
    )(c_col, w_ada, b_ada)


def _rms_modulate(x, w, sc, sh):
    y = x * lax.rsqrt(jnp.mean(x * x, axis=-1, keepdims=True) + NORM_EPS)
    return (y * w) * (1.0 + sc) + sh


def _head_norm_rope(t, w, cos, sin_e, sin_o, scale):
    t = t * lax.rsqrt(jnp.mean(t * t, axis=-1, keepdims=True) + NORM_EPS) * w
    nxt = pltpu.roll(t, HEAD_DIM - 1, 1)
    prv = pltpu.roll(t, 1, 1)
    return (t * cos + nxt * sin_e + prv * sin_o) * scale


def _inproj_kernel(x_ref, n1_ref, sc_ref, sh_ref, w_ref, qw_ref, kw_ref, cos_ref, se_ref, so_ref,
                   o_ref, h_ref, *, n_q_tiles, heads_per_tile, n_k_heads, q_scale):
    j = pl.program_id(1)

    @pl.when(j == 0)
    def _():
        h_ref[...] = _rms_modulate(x_ref[...], n1_ref[...], sc_ref[...], sh_ref[...]).astype(h_ref.dtype)

    acc = jnp.dot(h_ref[...], w_ref[...], preferred_element_type=jnp.float32)

    def rope_heads(n_heads, w, scale):
        cos, se, so = cos_ref[...], se_ref[...], so_ref[...]
        for hh in range(heads_per_tile):
            sl = slice(hh * HEAD_DIM, (hh + 1) * HEAD_DIM)
            t = acc[:, sl]
            if hh < n_heads:
                t = _head_norm_rope(t, w, cos, se, so, scale)
            o_ref[:, sl] = t.astype(o_ref.dtype)

    @pl.when(j < n_q_tiles)
    def _():
        rope_heads(heads_per_tile, qw_ref[...], q_scale)

    @pl.when(j == n_q_tiles)
    def _():
        rope_heads(n_k_heads, kw_ref[...], 1.0)

    @pl.when(j > n_q_tiles)
    def _():
        o_ref[...] = acc.astype(o_ref.dtype)


def _inproj(x, n1, sc1, sh1, w_in, qw, kw, cos, sin_e, sin_o, *, n_q_heads, n_k_heads, q_scale,
            tm=512, tn=512):
    s, d = x.shape
    n = w_in.shape[1]
    heads_per_tile = tn // HEAD_DIM
    assert n_q_heads % heads_per_tile == 0 and n_k_heads <= heads_per_tile
    kern = functools.partial(_inproj_kernel, n_q_tiles=n_q_heads // heads_per_tile,
                             heads_per_tile=heads_per_tile, n_k_heads=n_k_heads, q_scale=q_scale)
    row = lambda i, j: (0, 0)
    return pl.pallas_call(
        kern,
        grid=(s // tm, n // tn),
        in_specs=[pl.BlockSpec((tm, d), lambda i, j: (i, 0)),
                  pl.BlockSpec((1, d), row), pl.BlockSpec((1, d), row), pl.BlockSpec((1, d), row),
                  pl.BlockSpec((d, tn), lambda i, j: (0, j)),
                  pl.BlockSpec((1, HEAD_DIM), row), pl.BlockSpec((1, HEAD_DIM), row),
                  pl.BlockSpec((tm, HEAD_DIM), lambda i, j: (i, 0)),
                  pl.BlockSpec((tm, HEAD_DIM), lambda i, j: (i, 0)),
                  pl.BlockSpec((tm, HEAD_DIM), lambda i, j: (i, 0))],
        out_specs=pl.BlockSpec((tm, tn), lambda i, j: (i, j)),
        out_shape=jax.ShapeDtypeStruct((s, n), jnp.bfloat16),
        scratch_shapes=[pltpu.VMEM((tm, d), jnp.bfloat16)],
        compiler_params=_cparams(("parallel", "arbitrary")),
        name="inproj",
    )(x, n1, sc1, sh1, w_in, qw, kw, cos, sin_e, sin_o)


def _gqa_kernel(q_ref, k_ref, v_ref, o_ref, qs_ref, m_ref, l_ref, acc_ref, *, tq, tk, n_chunks):
    for g in range(KV_GROUP):
        qs_ref[g * tq:(g + 1) * tq, :] = q_ref[:, g * HEAD_DIM:(g + 1) * HEAD_DIM]
    m_ref[...] = jnp.full(m_ref.shape, -jnp.inf, jnp.float32)
    l_ref[...] = jnp.zeros(l_ref.shape, jnp.float32)
    acc_ref[...] = jnp.zeros(acc_ref.shape, jnp.float32)

    def body(c, carry):
        start = pl.multiple_of(c * tk, tk)
        k = k_ref[pl.ds(start, tk), :]
        v = v_ref[pl.ds(start, tk), :]
        s = lax.dot_general(qs_ref[...], k, _NT, preferred_element_type=jnp.float32)
        m_old = m_ref[...]
        m_new = jnp.maximum(m_old, jnp.max(s, axis=-1, keepdims=True))
        alpha = jnp.exp2(m_old - m_new)
        p = jnp.exp2(s - m_new)
        l_ref[...] = alpha * l_ref[...] + jnp.sum(p, axis=-1, keepdims=True)
        acc_ref[...] = alpha * acc_ref[...] + jnp.dot(p.astype(v.dtype), v, preferred_element_type=jnp.float32)
        m_ref[...] = m_new
        return carry

    lax.fori_loop(0, n_chunks, body, 0)
    out = acc_ref[...] / l_ref[...]
    for g in range(KV_GROUP):
        o_ref[:, g * HEAD_DIM:(g + 1) * HEAD_DIM] = out[g * tq:(g + 1) * tq, :].astype(o_ref.dtype)


def _gqa(proj, *, n_q_heads, n_kv_heads, k_col, v_col, tq=128, tk=512):
    s = proj.shape[0]
    gw = KV_GROUP * HEAD_DIM
    kern = functools.partial(_gqa_kernel, tq=tq, tk=tk, n_chunks=s // tk)
    return pl.pallas_call(
        kern,
        grid=(n_kv_heads, s // tq),
        in_specs=[pl.BlockSpec((tq, gw), lambda h, i: (i, h)),
                  pl.BlockSpec((s, HEAD_DIM), lambda h, i: (0, k_col // HEAD_DIM + h)),
                  pl.BlockSpec((s, HEAD_DIM), lambda h, i: (0, v_col // HEAD_DIM + h))],
        out_specs=pl.BlockSpec((tq, gw), lambda h, i: (i, h)),
        out_shape=jax.ShapeDtypeStruct((s, n_q_heads * HEAD_DIM), jnp.bfloat16),
        scratch_shapes=[pltpu.VMEM((KV_GROUP * tq, HEAD_DIM), jnp.bfloat16),
                        pltpu.VMEM((KV_GROUP * tq, 1), jnp.float32),
                        pltpu.VMEM((KV_GROUP * tq, 1), jnp.float32),
                        pltpu.VMEM((KV_GROUP * tq, HEAD_DIM), jnp.float32)],
        compiler_params=_cparams(("parallel", "arbitrary")),
        name="gqa",
    )(proj, proj, proj)


def _natten_kernel(q_ref, k_ref, v_ref, b_ref, o_ref, *, rows_per_step, n_rows, scale):
    rb = pl.program_id(1)
    win = NA_ROWS * GRID_W

    def body(i, carry):
        r = rb * rows_per_step + i
        rs = jnp.clip(r - NA_ROWS // 2, 0, n_rows - NA_ROWS)
        ds = rs - r + (NA_ROWS - 1)
        q0 = pl.multiple_of(i * GRID_W, GRID_W)
        k0 = pl.multiple_of(rs * GRID_W, GRID_W)
        q = q_ref[pl.ds(q0, GRID_W), :]
        kw = k_ref[pl.ds(k0, win), :]
        vw = v_ref[pl.ds(k0, win), :]
        s = lax.dot_general(q, kw, _NT, preferred_element_type=jnp.float32) * scale + b_ref[ds]
        m = jnp.max(s, axis=-1, keepdims=True)
        p = jnp.exp(s - m)
        l = jnp.sum(p, axis=-1, keepdims=True)
        o = jnp.dot(p.astype(vw.dtype), vw, preferred_element_type=jnp.float32) / l
        o_ref[pl.ds(q0, GRID_W), :] = o.astype(o_ref.dtype)
        return carry

    lax.fori_loop(0, rows_per_step, body, 0)


def _natten(proj, bias, *, n_heads, q_col, k_col, v_col, rows_per_step=8):
    s = proj.shape[0]
    n_rows = s // GRID_W
    tq = rows_per_step * GRID_W
    win = NA_ROWS * GRID_W
    kern = functools.partial(_natten_kernel, rows_per_step=rows_per_step, n_rows=n_rows,
                             scale=HEAD_DIM ** -0.5)
    return pl.pallas_call(
        kern,
        grid=(n_heads, n_rows // rows_per_step),
        in_specs=[pl.BlockSpec((tq, HEAD_DIM), lambda h, i: (i, q_col // HEAD_DIM + h)),
                  pl.BlockSpec((s, HEAD_DIM), lambda h, i: (0, k_col // HEAD_DIM + h)),
                  pl.BlockSpec((s, HEAD_DIM), lambda h, i: (0, v_col // HEAD_DIM + h)),
                  pl.BlockSpec((None, NA_ROWS, GRID_W, win), lambda h, i: (h, 0, 0, 0))],
        out_specs=pl.BlockSpec((tq, HEAD_DIM), lambda h, i: (i, h)),
        out_shape=jax.ShapeDtypeStruct((s, n_heads * HEAD_DIM), jnp.bfloat16),
        compiler_params=_cparams(("parallel", "arbitrary")),
        name="natten",
    )(proj, proj, proj, bias)


def _natten_bias(rpb):
    c = jnp.arange(GRID_W, dtype=jnp.int32)[:, None]
    kc = jnp.arange(GRID_W, dtype=jnp.int32)[None, :]
    cs = jnp.clip(c - NA_COLS // 2, 0, GRID_W - NA_COLS)
    valid = (kc >= cs) & (kc < cs + NA_COLS)
    rel_c = jnp.clip(kc - c + (NA_COLS - 1), 0, 2 * NA_COLS - 2)
    t = jnp.where(valid[None, None], rpb[:, :, rel_c], NEG_BIG)
    rows = jnp.arange(NA_ROWS)[:, None] + jnp.arange(NA_ROWS)[None, :]
    t = t[:, rows]
    t = t.transpose(0, 1, 3, 2, 4)
    return t.reshape(rpb.shape[0], NA_ROWS, GRID_W, NA_ROWS * GRID_W)


def _merge_kernel(ya_ref, yb_ref, ga_ref, gb_ref, woa_ref, wob_ref, wout_ref, x_ref, g1_ref, n2_ref, sc_ref,
                  sh_ref, x1_ref, h2_ref, m_ref, *, tn, n_tiles):
    n = pl.program_id(1)
    a = jnp.dot(ya_ref[...], woa_ref[...], preferred_element_type=jnp.float32)
    b = jnp.dot(yb_ref[...], wob_ref[...], preferred_element_type=jnp.float32)
    m = (jax.nn.sigmoid(ga_ref[...].astype(jnp.float32)) * a
         + jax.nn.sigmoid(gb_ref[...].astype(jnp.float32)) * b)
    col = pl.multiple_of(n * tn, tn)
    m_ref[:, pl.ds(col, tn)] = m.astype(m_ref.dtype)

    @pl.when(n == n_tiles - 1)
    def _():
        y = jnp.dot(m_ref[...], wout_ref[...], preferred_element_type=jnp.float32)
        x1 = x_ref[...] + g1_ref[...] * y
        x1_ref[...] = x1
        h2_ref[...] = _rms_modulate(x1, n2_ref[...], sc_ref[...], sh_ref[...]).astype(h2_ref.dtype)


def _merge(ya, yb, proj, w_oa, w_ob, w_out, x, g1, n2, sc2, sh2, *, ga_col, gb_col, tm=512, tn=512):
    s, d = x.shape
    da = ya.shape[1]
    n_tiles = d // tn
    kern = functools.partial(_merge_kernel, tn=tn, n_tiles=n_tiles)
    row = lambda i, n: (0, 0)
    return pl.pallas_call(
        kern,
        grid=(s // tm, n_tiles),
        in_specs=[pl.BlockSpec((tm, da), lambda i, n: (i, 0)),
                  pl.BlockSpec((tm, da), lambda i, n: (i, 0)),
                  pl.BlockSpec((tm, tn), lambda i, n: (i, ga_col // tn + n)),
                  pl.BlockSpec((tm, tn), lambda i, n: (i, gb_col // tn + n)),
                  pl.BlockSpec((da, tn), lambda i, n: (0, n)),
                  pl.BlockSpec((da, tn), lambda i, n: (0, n)),
                  pl.BlockSpec((d, d), row, pipeline_mode=pl.Buffered(1)),
                  pl.BlockSpec((tm, d), lambda i, n: (i, 0)),
                  pl.BlockSpec((1, d), row), pl.BlockSpec((1, d), row),
                  pl.BlockSpec((1, d), row), pl.BlockSpec((1, d), row)],
        out_specs=[pl.BlockSpec((tm, d), lambda i, n: (i, 0)),
                   pl.BlockSpec((tm, d), lambda i, n: (i, 0))],
        out_shape=[jax.ShapeDtypeStruct((s, d), jnp.float32),
                   jax.ShapeDtypeStruct((s, d), jnp.bfloat16)],
        scratch_shapes=[pltpu.VMEM((tm, d), jnp.bfloat16)],
        compiler_params=_cparams(("parallel", "arbitrary")),
        name="merge",
    )(ya, yb, proj, proj, w_oa, w_ob, w_out, x, g1, n2, sc2, sh2)


def _ffn_kernel(h_ref, wg_ref, wu_ref, wd_ref, x1_ref, g2_ref, fw_ref, o_ref, acc_ref, *, n_tiles, final_norm):
    f = pl.program_id(1)

    @pl.when(f == 0)
    def _():
        acc_ref[...] = jnp.zeros(acc_ref.shape, jnp.float32)

    h = h_ref[...]
    g = jnp.dot(h, wg_ref[...], preferred_element_type=jnp.float32)
    u = jnp.dot(h, wu_ref[...], preferred_element_type=jnp.float32)
    a = (g * jax.nn.sigmoid(g) * u).astype(h.dtype)
    acc_ref[...] += jnp.dot(a, wd_ref[...], preferred_element_type=jnp.float32)

    @pl.when(f == n_tiles - 1)
    def _():
        x2 = x1_ref[...] + g2_ref[...] * acc_ref[...]
        if final_norm:
            x2 = x2 * lax.rsqrt(jnp.mean(x2 * x2, axis=-1, keepdims=True) + NORM_EPS) * fw_ref[...]
        o_ref[...] = x2


def _ffn(h2, w_gate, w_up, w_down, x1, g2, final_w, *, final_norm, tm=512, tf=512):
    s, d = x1.shape
    dff = w_gate.shape[1]
    n_tiles = dff // tf
    kern = functools.partial(_ffn_kernel, n_tiles=n_tiles, final_norm=final_norm)
    row = lambda i, f: (0, 0)
    return pl.pallas_call(
        kern,
        grid=(s // tm, n_tiles),
        in_specs=[pl.BlockSpec((tm, d), lambda i, f: (i, 0)),
                  pl.BlockSpec((d, tf), lambda i, f: (0, f)),
                  pl.BlockSpec((d, tf), lambda i, f: (0, f)),
                  pl.BlockSpec((tf, d), lambda i, f: (f, 0)),
                  pl.BlockSpec((tm, d), lambda i, f: (i, 0)),
                  pl.BlockSpec((1, d), row), pl.BlockSpec((1, d), row)],
        out_specs=pl.BlockSpec((tm, d), lambda i, f: (i, 0)),
        out_shape=jax.ShapeDtypeStruct((s, d), jnp.float32),
        scratch_shapes=[pltpu.VMEM((tm, d), jnp.float32)],
        compiler_params=_cparams(("parallel", "arbitrary")),
        name="ffn",
    )(h2, w_gate, w_up, w_down, x1, g2, final_w)


def _rope_tables(s):
    t = jnp.arange(s, dtype=jnp.int32)
    row = (t // GRID_W).astype(jnp.float32)
    col = (t % GRID_W).astype(jnp.float32)
    axis_dim = HEAD_DIM // 2
    inv = ROPE_THETA ** (-jnp.arange(0, axis_dim, 2, dtype=jnp.float32) / axis_dim)
    ang = jnp.concatenate([row[:, None] * inv[None], col[:, None] * inv[None]], axis=-1)
    cos = jnp.repeat(jnp.cos(ang), 2, axis=-1)
    sin = jnp.repeat(jnp.sin(ang), 2, axis=-1)
    even = (jnp.arange(HEAD_DIM) % 2 == 0)[None, :]
    return cos, jnp.where(even, -sin, 0.0), jnp.where(even, 0.0, sin)


def kernel(x, c, w_ada, b_ada, norm1_w, w_in, q_norm_w, k_norm_w, nat_rpb, w_oa, w_ob, w_out, norm2_w,
           w_ffn_gate, w_ffn_up, w_ffn_down, final_w):
    b, s, d = x.shape
    assert b == 1
    depth = w_ada.shape[0]
    n_heads_a = w_oa.shape[1] // HEAD_DIM
    n_heads_b = w_ob.shape[1] // HEAD_DIM
    n_kv_a = n_heads_a // KV_GROUP
    splits = (n_heads_a * HEAD_DIM, n_kv_a * HEAD_DIM, n_kv_a * HEAD_DIM,
              n_heads_b * HEAD_DIM, n_heads_b * HEAD_DIM, n_heads_b * HEAD_DIM, d, d)
    offs = [0] + np.cumsum(splits)[:-1].tolist()
    qa_col, ka_col, va_col, qb_col, kb_col, vb_col, ga_col, gb_col = offs
    assert qa_col == 0

    bf = jnp.bfloat16
    cos, sin_e, sin_o = _rope_tables(s)
    xs = x[0]
    c_col = c.reshape(d, 1)
    for l in range(depth):
        mod = _adaln(c_col, w_ada[l], b_ada[l][None, :])
        sh1, sc1, g1, sh2, sc2, g2 = [mod[:, i * d:(i + 1) * d] for i in range(6)]
        proj = _inproj(xs, norm1_w[l][None, :], sc1, sh1, w_in[l].astype(bf),
                       q_norm_w[l][None, :], k_norm_w[l][None, :], cos, sin_e, sin_o,
                       n_q_heads=n_heads_a, n_k_heads=n_kv_a, q_scale=HEAD_DIM ** -0.5 * LOG2E)
        ya = _gqa(proj, n_q_heads=n_heads_a, n_kv_heads=n_kv_a, k_col=ka_col, v_col=va_col)
        yb = _natten(proj, _natten_bias(nat_rpb[l]), n_heads=n_heads_b,
                     q_col=qb_col, k_col=kb_col, v_col=vb_col)
        xs, h2 = _merge(ya, yb, proj, w_oa[l].astype(bf), w_ob[l].astype(bf), w_out[l].astype(bf),
                        xs, g1, norm2_w[l][None, :], sc2, sh2, ga_col=ga_col, gb_col=gb_col)
        xs = _ffn(h2, w_ffn_gate[l].astype(bf), w_ffn_up[l].astype(bf), w_ffn_down[l].astype(bf),
                  xs, g2, final_w[None, :], final_norm=(l == depth - 1))
    return xs[None]
```

```python
import functools
import math

import jax
import jax.numpy as jnp
import numpy as np
from jax import lax
from jax.experimental import pallas as pl
from jax.experimental.pallas import tpu as pltpu

HEAD_DIM = 128
KV_GROUP = 4
GRID_W = 64
NA_ROWS = 8
NA_COLS = 16
ROPE_THETA = 10000.0
NORM_EPS = 1e-6
NEG_BIG = -1e30
LOG2E = 1.4426950408889634

VMEM_LIMIT = 56 * 1024 * 1024

_NT = (((1,), (1,)), ((), ()))


def _cparams(sem):
    return pltpu.CompilerParams(dimension_semantics=sem, vmem_limit_bytes=VMEM_LIMIT)


def _adaln_kernel(c_ref, w_ref, b_ref, o_ref):
    c = c_ref[...]
    act = c * jax.nn.sigmoid(c)
    o_ref[...] = jnp.sum(act * w_ref[...], axis=0, keepdims=True) + b_ref[...]


def _adaln(c_col, w_ada, b_ada, tn=1024):
    d, n = w_ada.shape
    return pl.pallas_call(
        _adaln_kernel,
        grid=(n // tn,),
        in_specs=[pl.BlockSpec((d, 1), lambda j: (0, 0)),
                  pl.BlockSpec((d, tn), lambda j: (0, j)),
                  pl.BlockSpec((1, tn), lambda j: (0, j))],
        out_specs=pl.BlockSpec((1, tn), lambda j: (0, j)),
        out_shape=jax.ShapeDtypeStruct((1, n), jnp.float32),
        compiler_params=_cparams(("arbitrary",)),
        name="adaln",
    )(c_col, w_ada, b_ada)


def _rms_modulate(x, w, sc, sh):
    y = x * lax.rsqrt(jnp.mean(x * x, axis=-1, keepdims=True) + NORM_EPS)
    return (y * w) * (1.0 + sc) + sh


def _head_norm_rope(t, w, cos, sin_e, sin_o, scale):
    t = t * lax.rsqrt(jnp.mean(t * t, axis=-1, keepdims=True) + NORM_EPS) * w
    nxt = pltpu.roll(t, HEAD_DIM - 1, 1)
    prv = pltpu.roll(t, 1, 1)
    return (t * cos + nxt * sin_e + prv * sin_o) * scale


def _inproj_kernel(x_ref, n1_ref, sc_ref, sh_ref, w_ref, qw_ref, kw_ref, cos_ref, se_ref, so_ref,
                   o_ref, h_ref, *, n_q_tiles, heads_per_tile, n_k_heads, qb_tiles, q_scale):
    j = pl.program_id(1)

    @pl.when(j == 0)
    def _():
        h_ref[...] = _rms_modulate(x_ref[...], n1_ref[...], sc_ref[...], sh_ref[...]).astype(h_ref.dtype)

    acc = jnp.dot(h_ref[...], w_ref[...], preferred_element_type=jnp.float32)

    def rope_heads(n_heads, w, scale):
        cos, se, so = cos_ref[...], se_ref[...], so_ref[...]
        for hh in range(heads_per_tile):
            sl = slice(hh * HEAD_DIM, (hh + 1) * HEAD_DIM)
            t = acc[:, sl]
            if hh < n_heads:
                t = _head_norm_rope(t, w, cos, se, so, scale)
            o_ref[:, sl] = t.astype(o_ref.dtype)

    @pl.when(j < n_q_tiles)
    def _():
        rope_heads(heads_per_tile, qw_ref[...], q_scale)

    @pl.when(j == n_q_tiles)
    def _():
        rope_heads(n_k_heads, kw_ref[...], 1.0)

    in_qb = (j >= qb_tiles[0]) & (j < qb_tiles[1])

    @pl.when(in_qb)
    def _():
        o_ref[...] = (acc * q_scale).astype(o_ref.dtype)

    @pl.when((j > n_q_tiles) & jnp.logical_not(in_qb))
    def _():
        o_ref[...] = acc.astype(o_ref.dtype)


def _inproj(x, n1, sc1, sh1, w_in, qw, kw, cos, sin_e, sin_o, *, n_q_heads, n_k_heads, qb_cols, q_scale,
            tm=512, tn=512):
    s, d = x.shape
    n = w_in.shape[1]
    heads_per_tile = tn // HEAD_DIM
    assert n_q_heads % heads_per_tile == 0 and n_k_heads <= heads_per_tile
    assert qb_cols[0] % tn == 0 and qb_cols[1] % tn == 0 and qb_cols[0] // tn > n_q_heads // heads_per_tile
    kern = functools.partial(_inproj_kernel, n_q_tiles=n_q_heads // heads_per_tile,
                             heads_per_tile=heads_per_tile, n_k_heads=n_k_heads,
                             qb_tiles=(qb_cols[0] // tn, qb_cols[1] // tn), q_scale=q_scale)
    row = lambda i, j: (0, 0)
    return pl.pallas_call(
        kern,
        grid=(s // tm, n // tn),
        in_specs=[pl.BlockSpec((tm, d), lambda i, j: (i, 0)),
                  pl.BlockSpec((1, d), row), pl.BlockSpec((1, d), row), pl.BlockSpec((1, d), row),
                  pl.BlockSpec((d, tn), lambda i, j: (0, j)),
                  pl.BlockSpec((1, HEAD_DIM), row), pl.BlockSpec((1, HEAD_DIM), row),
                  pl.BlockSpec((tm, HEAD_DIM), lambda i, j: (i, 0)),
                  pl.BlockSpec((tm, HEAD_DIM), lambda i, j: (i, 0)),
                  pl.BlockSpec((tm, HEAD_DIM), lambda i, j: (i, 0))],
        out_specs=pl.BlockSpec((tm, tn), lambda i, j: (i, j)),
        out_shape=jax.ShapeDtypeStruct((s, n), jnp.bfloat16),
        scratch_shapes=[pltpu.VMEM((tm, d), jnp.bfloat16)],
        compiler_params=_cparams(("parallel", "arbitrary")),
        name="inproj",
    )(x, n1, sc1, sh1, w_in, qw, kw, cos, sin_e, sin_o)


def _gqa_kernel(q_ref, k_ref, v_ref, o_ref, qs_ref, s0_ref, s1_ref, p0_ref, p1_ref, a0_ref, a1_ref,
                m_ref, l_ref, acc_ref, *, tq, tk, n_chunks):
    s_refs, p_refs, a_refs = (s0_ref, s1_ref), (p0_ref, p1_ref), (a0_ref, a1_ref)
    n_col = tk // HEAD_DIM
    for g in range(KV_GROUP):
        qs_ref[g * tq:(g + 1) * tq, :] = q_ref[:, g * HEAD_DIM:(g + 1) * HEAD_DIM]
    m_ref[...] = jnp.full(m_ref.shape, -jnp.inf, jnp.float32)
    l_ref[...] = jnp.zeros(l_ref.shape, jnp.float32)
    acc_ref[...] = jnp.zeros(acc_ref.shape, jnp.float32)

    def scores(c, slot):
        k = k_ref[pl.ds(pl.multiple_of(c * tk, tk), tk), :]
        s_refs[slot][...] = lax.dot_general(qs_ref[...], k, _NT, preferred_element_type=jnp.float32)

    def softmax(slot):
        s_ref, p_ref = s_refs[slot], p_refs[slot]
        blocks = [s_ref[:, j * HEAD_DIM:(j + 1) * HEAD_DIM] for j in range(n_col)]
        mx = functools.reduce(jnp.maximum, blocks)
        m_old = m_ref[...]
        m_new = jnp.maximum(m_old, jnp.max(mx, axis=-1, keepdims=True))
        alpha = jnp.exp2(m_old - m_new)
        psum = None
        for j in range(n_col):
            p = jnp.exp2(blocks[j] - m_new)
            p_ref[:, j * HEAD_DIM:(j + 1) * HEAD_DIM] = p.astype(p_ref.dtype)
            psum = p if psum is None else psum + p
        l_ref[...] = alpha * l_ref[...] + psum
        a_refs[slot][...] = alpha
        m_ref[...] = m_new

    def weighted_values(c, slot):
        v = v_ref[pl.ds(pl.multiple_of(c * tk, tk), tk), :]
        acc_ref[...] = a_refs[slot][...] * acc_ref[...] + jnp.dot(
            p_refs[slot][...], v, preferred_element_type=jnp.float32)

    assert n_chunks % 2 == 0 and n_chunks >= 4
    scores(0, 0)
    scores(1, 1)
    softmax(0)

    def body(i, carry):
        t = 2 * i + 1
        scores(t + 1, 0)
        softmax(1)
        weighted_values(t - 1, 0)
        scores(t + 2, 1)
        softmax(0)
        weighted_values(t, 1)
        return carry

    lax.fori_loop(0, (n_chunks - 2) // 2, body, 0)
    softmax(1)
    weighted_values(n_chunks - 2, 0)
    weighted_values(n_chunks - 1, 1)
    out = acc_ref[...] / jnp.sum(l_ref[...], axis=-1, keepdims=True)
    for g in range(KV_GROUP):
        o_ref[:, g * HEAD_DIM:(g + 1) * HEAD_DIM] = out[g * tq:(g + 1) * tq, :].astype(o_ref.dtype)


def _gqa(proj, *, n_q_heads, n_kv_heads, k_col, v_col, tq=128, tk=512):
    s = proj.shape[0]
    gw = KV_GROUP * HEAD_DIM
    rows = KV_GROUP * tq
    kern = functools.partial(_gqa_kernel, tq=tq, tk=tk, n_chunks=s // tk)
    stat = pltpu.VMEM((rows, HEAD_DIM), jnp.float32)
    return pl.pallas_call(
        kern,
        grid=(n_kv_heads, s // tq),
        in_specs=[pl.BlockSpec((tq, gw), lambda h, i: (i, h)),
                  pl.BlockSpec((s, HEAD_DIM), lambda h, i: (0, k_col // HEAD_DIM + h)),
                  pl.BlockSpec((s, HEAD_DIM), lambda h, i: (0, v_col // HEAD_DIM + h))],
        out_specs=pl.BlockSpec((tq, gw), lambda h, i: (i, h)),
        out_shape=jax.ShapeDtypeStruct((s, n_q_heads * HEAD_DIM), jnp.bfloat16),
        scratch_shapes=[pltpu.VMEM((rows, HEAD_DIM), jnp.bfloat16),
                        pltpu.VMEM((rows, tk), jnp.float32), pltpu.VMEM((rows, tk), jnp.float32),
                        pltpu.VMEM((rows, tk), jnp.bfloat16), pltpu.VMEM((rows, tk), jnp.bfloat16),
                        stat, stat, stat, stat, stat],
        compiler_params=_cparams(("parallel", "arbitrary")),
        name="gqa",
    )(proj, proj, proj)


NA_GROUP = 4
NA_WIN_ROWS = NA_GROUP + NA_ROWS


def _na_window_start(r0, n_rows):
    return jnp.clip(r0 - NA_ROWS // 2, 0, n_rows - NA_WIN_ROWS)


def _natten_kernel(q_ref, k_ref, v_ref, b_ref, o_ref, *, groups_per_step, n_rows):
    gq = NA_GROUP * GRID_W
    win = NA_WIN_ROWS * GRID_W
    for g in range(groups_per_step):
        r0 = (pl.program_id(1) * groups_per_step + g) * NA_GROUP
        variant = jnp.where(r0 == 0, 0, jnp.where(r0 == n_rows - NA_GROUP, 2, 1))
        k0 = pl.multiple_of(_na_window_start(r0, n_rows) * GRID_W, GRID_W)
        q = q_ref[g * gq:(g + 1) * gq, :]
        kw = k_ref[pl.ds(k0, win), :]
        vw = v_ref[pl.ds(k0, win), :]
        s = lax.dot_general(q, kw, _NT, preferred_element_type=jnp.float32) + b_ref[variant]
        p = jnp.exp2(s - jnp.max(s, axis=-1, keepdims=True))
        l = jnp.sum(p, axis=-1, keepdims=True)
        o = jnp.dot(p.astype(vw.dtype), vw, preferred_element_type=jnp.float32) / l
        o_ref[g * gq:(g + 1) * gq, :] = o.astype(o_ref.dtype)


def _natten(proj, bias, *, n_heads, q_col, k_col, v_col, groups_per_step=4):
    s = proj.shape[0]
    n_rows = s // GRID_W
    tq = groups_per_step * NA_GROUP * GRID_W
    assert n_rows % (groups_per_step * NA_GROUP) == 0 and n_rows >= 2 * NA_WIN_ROWS
    kern = functools.partial(_natten_kernel, groups_per_step=groups_per_step, n_rows=n_rows)
    return pl.pallas_call(
        kern,
        grid=(n_heads, s // tq),
        in_specs=[pl.BlockSpec((tq, HEAD_DIM), lambda h, i: (i, q_col // HEAD_DIM + h)),
                  pl.BlockSpec((s, HEAD_DIM), lambda h, i: (0, k_col // HEAD_DIM + h)),
                  pl.BlockSpec((s, HEAD_DIM), lambda h, i: (0, v_col // HEAD_DIM + h)),
                  pl.BlockSpec((None,) + bias.shape[1:], lambda h, i: (h, 0, 0, 0))],
        out_specs=pl.BlockSpec((tq, HEAD_DIM), lambda h, i: (i, h)),
        out_shape=jax.ShapeDtypeStruct((s, n_heads * HEAD_DIM), jnp.bfloat16),
        compiler_params=_cparams(("parallel", "arbitrary")),
        name="natten",
    )(proj, proj, proj, bias)


def _natten_bias(rpb, n_rows):
    c = np.arange(GRID_W)[:, None]
    kc = np.arange(GRID_W)[None, :]
    cs = np.clip(c - NA_COLS // 2, 0, GRID_W - NA_COLS)
    col_ok = (kc >= cs) & (kc < cs + NA_COLS)
    rel_c = np.clip(kc - c + (NA_COLS - 1), 0, 2 * NA_COLS - 2)
    i = np.arange(NA_GROUP)[None, :, None]
    wr = np.arange(NA_WIN_ROWS)[None, None, :]
    r0 = np.array([0, NA_WIN_ROWS, n_rows - NA_GROUP])[:, None, None]
    ws = np.clip(r0 - NA_ROWS // 2, 0, n_rows - NA_WIN_ROWS)
    rs = np.clip(r0 + i - NA_ROWS // 2, 0, n_rows - NA_ROWS)
    row_ok = (ws + wr >= rs) & (ws + wr < rs + NA_ROWS)
    rel_r = np.clip(ws + wr - (r0 + i) + (NA_ROWS - 1), 0, 2 * NA_ROWS - 2)
    t = rpb[:, rel_r][:, :, :, :, rel_c] * LOG2E
    ok = row_ok[:, :, :, None, None] & col_ok[None, None, None]
    t = jnp.where(ok[None], t, NEG_BIG)
    t = t.transpose(0, 1, 2, 4, 3, 5)
    return t.reshape(rpb.shape[0], 3, NA_GROUP * GRID_W, NA_WIN_ROWS * GRID_W)


def _merge_kernel(ya_ref, yb_ref, ga_ref, gb_ref, woa_ref, wob_ref, wout_ref, x_ref, g1_ref, n2_ref, sc_ref,
                  sh_ref, x1_ref, h2_ref, m_ref, *, tn, n_tiles):
    n = pl.program_id(1)
    a = jnp.dot(ya_ref[...], woa_ref[...], preferred_element_type=jnp.float32)
    b = jnp.dot(yb_ref[...], wob_ref[...], preferred_element_type=jnp.float32)
    m = (jax.nn.sigmoid(ga_ref[...].astype(jnp.float32)) * a
         + jax.nn.sigmoid(gb_ref[...].astype(jnp.float32)) * b)
    col = pl.multiple_of(n * tn, tn)
    m_ref[:, pl.ds(col, tn)] = m.astype(m_ref.dtype)

    @pl.when(n == n_tiles - 1)
    def _():
        y = jnp.dot(m_ref[...], wout_ref[...], preferred_element_type=jnp.float32)
        x1 = x_ref[...] + g1_ref[...] * y
        x1_ref[...] = x1
        h2_ref[...] = _rms_modulate(x1, n2_ref[...], sc_ref[...], sh_ref[...]).astype(h2_ref.dtype)


def _merge(ya, yb, proj, w_oa, w_ob, w_out, x, g1, n2, sc2, sh2, *, ga_col, gb_col, tm=512, tn=512):
    s, d = x.shape
    da = ya.shape[1]
    n_tiles = d // tn
    kern = functools.partial(_merge_kernel, tn=tn, n_tiles=n_tiles)
    row = lambda i, n: (0, 0)
    return pl.pallas_call(
        kern,
        grid=(s // tm, n_tiles),
        in_specs=[pl.BlockSpec((tm, da), lambda i, n: (i, 0)),
                  pl.BlockSpec((tm, da), lambda i, n: (i, 0)),
                  pl.BlockSpec((tm, tn), lambda i, n: (i, ga_col // tn + n)),
                  pl.BlockSpec((tm, tn), lambda i, n: (i, gb_col // tn + n)),
                  pl.BlockSpec((da, tn), lambda i, n: (0, n)),
                  pl.BlockSpec((da, tn), lambda i, n: (0, n)),
                  pl.BlockSpec((d, d), row, pipeline_mode=pl.Buffered(1)),
                  pl.BlockSpec((tm, d), lambda i, n: (i, 0)),
                  pl.BlockSpec((1, d), row), pl.BlockSpec((1, d), row),
                  pl.BlockSpec((1, d), row), pl.BlockSpec((1, d), row)],
        out_specs=[pl.BlockSpec((tm, d), lambda i, n: (i, 0)),
                   pl.BlockSpec((tm, d), lambda i, n: (i, 0))],
        out_shape=[jax.ShapeDtypeStruct((s, d), jnp.float32),
                   jax.ShapeDtypeStruct((s, d), jnp.bfloat16)],
        scratch_shapes=[pltpu.VMEM((tm, d), jnp.bfloat16)],
        compiler_params=_cparams(("parallel", "arbitrary")),
        name="merge",
    )(ya, yb, proj, proj, w_oa, w_ob, w_out, x, g1, n2, sc2, sh2)


def _ffn_kernel(h_ref, wg_ref, wu_ref, wd_ref, x1_ref, g2_ref, fw_ref, o_ref, acc_ref, *, n_tiles, final_norm):
    f = pl.program_id(1)

    @pl.when(f == 0)
    def _():
        acc_ref[...] = jnp.zeros(acc_ref.shape, jnp.float32)

    h = h_ref[...]
    g = jnp.dot(h, wg_ref[...], preferred_element_type=jnp.float32)
    u = jnp.dot(h, wu_ref[...], preferred_element_type=jnp.float32)
    a = (g * jax.nn.sigmoid(g) * u).astype(h.dtype)
    acc_ref[...] += jnp.dot(a, wd_ref[...], preferred_element_type=jnp.float32)

    @pl.when(f == n_tiles - 1)
    def _():
        x2 = x1_ref[...] + g2_ref[...] * acc_ref[...]
        if final_norm:
            x2 = x2 * lax.rsqrt(jnp.mean(x2 * x2, axis=-1, keepdims=True) + NORM_EPS) * fw_ref[...]
        o_ref[...] = x2


def _ffn(h2, w_gate, w_up, w_down, x1, g2, final_w, *, final_norm, tm=512, tf=512):
    s, d = x1.shape
    dff = w_gate.shape[1]
    n_tiles = dff // tf
    kern = functools.partial(_ffn_kernel, n_tiles=n_tiles, final_norm=final_norm)
    row = lambda i, f: (0, 0)
    return pl.pallas_call(
        kern,
        grid=(s // tm, n_tiles),
        in_specs=[pl.BlockSpec((tm, d), lambda i, f: (i, 0)),
                  pl.BlockSpec((d, tf), lambda i, f: (0, f)),
                  pl.BlockSpec((d, tf), lambda i, f: (0, f)),
                  pl.BlockSpec((tf, d), lambda i, f: (f, 0)),
                  pl.BlockSpec((tm, d), lambda i, f: (i, 0)),
                  pl.BlockSpec((1, d), row), pl.BlockSpec((1, d), row)],
        out_specs=pl.BlockSpec((tm, d), lambda i, f: (i, 0)),
        out_shape=jax.ShapeDtypeStruct((s, d), jnp.float32),
        scratch_shapes=[pltpu.VMEM((tm, d), jnp.float32)],
        compiler_params=_cparams(("parallel", "arbitrary")),
        name="ffn",
    )(h2, w_gate, w_up, w_down, x1, g2, final_w)


def _rope_tables(s):
    t = jnp.arange(s, dtype=jnp.int32)
    row = (t // GRID_W).astype(jnp.float32)
    col = (t % GRID_W).astype(jnp.float32)
    axis_dim = HEAD_DIM // 2
    inv = ROPE_THETA ** (-jnp.arange(0, axis_dim, 2, dtype=jnp.float32) / axis_dim)
    ang = jnp.concatenate([row[:, None] * inv[None], col[:, None] * inv[None]], axis=-1)
    cos = jnp.repeat(jnp.cos(ang), 2, axis=-1)
    sin = jnp.repeat(jnp.sin(ang), 2, axis=-1)
    even = (jnp.arange(HEAD_DIM) % 2 == 0)[None, :]
    return cos, jnp.where(even, -sin, 0.0), jnp.where(even, 0.0, sin)


def kernel(x, c, w_ada, b_ada, norm1_w, w_in, q_norm_w, k_norm_w, nat_rpb, w_oa, w_ob, w_out, norm2_w,
           w_ffn_gate, w_ffn_up, w_ffn_down, final_w):
    b, s, d = x.shape
    assert b == 1
    depth = w_ada.shape[0]
    n_heads_a = w_oa.shape[1] // HEAD_DIM
    n_heads_b = w_ob.shape[1] // HEAD_DIM
    n_kv_a = n_heads_a // KV_GROUP
    splits = (n_heads_a * HEAD_DIM, n_kv_a * HEAD_DIM, n_kv_a * HEAD_DIM,
              n_heads_b * HEAD_DIM, n_heads_b * HEAD_DIM, n_heads_b * HEAD_DIM, d, d)
    offs = [0] + np.cumsum(splits)[:-1].tolist()
    qa_col, ka_col, va_col, qb_col, kb_col, vb_col, ga_col, gb_col = offs
    assert qa_col == 0

    bf = jnp.bfloat16
    cos, sin_e, sin_o = _rope_tables(s)
    xs = x[0]
    c_col = c.reshape(d, 1)
    for l in range(depth):
        mod = _adaln(c_col, w_ada[l], b_ada[l][None, :])
        sh1, sc1, g1, sh2, sc2, g2 = [mod[:, i * d:(i + 1) * d] for i in range(6)]
        proj = _inproj(xs, norm1_w[l][None, :], sc1, sh1, w_in[l].astype(bf),
                       q_norm_w[l][None, :], k_norm_w[l][None, :], cos, sin_e, sin_o,
                       n_q_heads=n_heads_a, n_k_heads=n_kv_a, qb_cols=(qb_col, kb_col),
                       q_scale=HEAD_DIM ** -0.5 * LOG2E)
        ya = _gqa(proj, n_q_heads=n_heads_a, n_kv_heads=n_kv_a, k_col=ka_col, v_col=va_col)
        yb = _natten(proj, _natten_bias(nat_rpb[l], s // GRID_W), n_heads=n_heads_b,
                     q_col=qb_col, k_col=kb_col, v_col=vb_col)
        xs, h2 = _merge(ya, yb, proj, w_oa[l].astype(bf), w_ob[l].astype(bf), w_out[l].astype(bf),
                        xs, g1, norm2_w[l][None, :], sc2, sh2, ga_col=ga_col, gb_col=gb_col)
        xs = _ffn(h2, w_ffn_gate[l].astype(bf), w_ffn_up[l].astype(bf), w_ffn_down[l].astype(bf),
                  xs, g2, final_w[None, :], final_norm=(l == depth - 1))
    return xs[None]
```

```python
import functools
import math

import jax
import jax.numpy as jnp
import numpy as np
from jax import lax
from jax.experimental import pallas as pl
from jax.experimental.pallas import tpu as pltpu

HEAD_DIM = 128
KV_GROUP = 4
GRID_W = 64
NA_ROWS = 8
NA_COLS = 16
ROPE_THETA = 10000.0
NORM_EPS = 1e-6
NEG_BIG = -1e30
LOG2E = 1.4426950408889634

VMEM_LIMIT = 56 * 1024 * 1024

_NT = (((1,), (1,)), ((), ()))


def _cparams(sem):
    return pltpu.CompilerParams(dimension_semantics=sem, vmem_limit_bytes=VMEM_LIMIT)


def _adaln_kernel(c_ref, w_ref, b_ref, o_ref):
    c = c_ref[...]
    act = c * jax.nn.sigmoid(c)
    o_ref[...] = jnp.sum(act * w_ref[...], axis=0, keepdims=True) + b_ref[...]


def _adaln(c_col, w_ada, b_ada, tn=1024):
    d, n = w_ada.shape
    return pl.pallas_call(
        _adaln_kernel,
        grid=(n // tn,),
        in_specs=[pl.BlockSpec((d, 1), lambda j: (0, 0)),
                  pl.BlockSpec((d, tn), lambda j: (0, j)),
                  pl.BlockSpec((1, tn), lambda j: (0, j))],
        out_specs=pl.BlockSpec((1, tn), lambda j: (0, j)),
        out_shape=jax.ShapeDtypeStruct((1, n), jnp.float32),
        compiler_params=_cparams(("arbitrary",)),
        name="adaln",
    )(c_col, w_ada, b_ada)


def _rms_modulate(x, w, sc, sh):
    y = x * lax.rsqrt(jnp.mean(x * x, axis=-1, keepdims=True) + NORM_EPS)
    return (y * w) * (1.0 + sc) + sh


def _head_norm_rope(t, w, cos, sin_e, sin_o, scale):
    t = t * lax.rsqrt(jnp.mean(t * t, axis=-1, keepdims=True) + NORM_EPS) * w
    nxt = pltpu.roll(t, HEAD_DIM - 1, 1)
    prv = pltpu.roll(t, 1, 1)
    return (t * cos + nxt * sin_e + prv * sin_o) * scale


def _inproj_kernel(x_ref, n1_ref, sc_ref, sh_ref, w_ref, qw_ref, kw_ref, cos_ref, se_ref, so_ref,
                   o_ref, h_ref, *, n_q_tiles, heads_per_tile, n_k_heads, qb_tiles, q_scale):
    j = pl.program_id(1)

    @pl.when(j == 0)
    def _():
        h_ref[...] = _rms_modulate(x_ref[...], n1_ref[...], sc_ref[...], sh_ref[...]).astype(h_ref.dtype)

    acc = jnp.dot(h_ref[...], w_ref[...], preferred_element_type=jnp.float32)

    def rope_heads(n_heads, w, scale):
        cos, se, so = cos_ref[...], se_ref[...], so_ref[...]
        for hh in range(heads_per_tile):
            sl = slice(hh * HEAD_DIM, (hh + 1) * HEAD_DIM)
            t = acc[:, sl]
            if hh < n_heads:
                t = _head_norm_rope(t, w, cos, se, so, scale)
            o_ref[:, sl] = t.astype(o_ref.dtype)

    @pl.when(j < n_q_tiles)
    def _():
        rope_heads(heads_per_tile, qw_ref[...], q_scale)

    @pl.when(j == n_q_tiles)
    def _():
        rope_heads(n_k_heads, kw_ref[...], 1.0)

    in_qb = (j >= qb_tiles[0]) & (j < qb_tiles[1])

    @pl.when(in_qb)
    def _():
        o_ref[...] = (acc * q_scale).astype(o_ref.dtype)

    @pl.when((j > n_q_tiles) & jnp.logical_not(in_qb))
    def _():
        o_ref[...] = acc.astype(o_ref.dtype)


def _inproj(x, n1, sc1, sh1, w_in, qw, kw, cos, sin_e, sin_o, *, n_q_heads, n_k_heads, qb_cols, q_scale,
            tm=1024, tn=512):
    s, d = x.shape
    n = w_in.shape[1]
    heads_per_tile = tn // HEAD_DIM
    assert n_q_heads % heads_per_tile == 0 and n_k_heads <= heads_per_tile
    assert qb_cols[0] % tn == 0 and qb_cols[1] % tn == 0 and qb_cols[0] // tn > n_q_heads // heads_per_tile
    kern = functools.partial(_inproj_kernel, n_q_tiles=n_q_heads // heads_per_tile,
                             heads_per_tile=heads_per_tile, n_k_heads=n_k_heads,
                             qb_tiles=(qb_cols[0] // tn, qb_cols[1] // tn), q_scale=q_scale)
    row = lambda i, j: (0, 0)
    return pl.pallas_call(
        kern,
        grid=(s // tm, n // tn),
        in_specs=[pl.BlockSpec((tm, d), lambda i, j: (i, 0)),
                  pl.BlockSpec((1, d), row), pl.BlockSpec((1, d), row), pl.BlockSpec((1, d), row),
                  pl.BlockSpec((d, tn), lambda i, j: (0, j)),
                  pl.BlockSpec((1, HEAD_DIM), row), pl.BlockSpec((1, HEAD_DIM), row),
                  pl.BlockSpec((tm, HEAD_DIM), lambda i, j: (i, 0)),
                  pl.BlockSpec((tm, HEAD_DIM), lambda i, j: (i, 0)),
                  pl.BlockSpec((tm, HEAD_DIM), lambda i, j: (i, 0))],
        out_specs=pl.BlockSpec((tm, tn), lambda i, j: (i, j)),
        out_shape=jax.ShapeDtypeStruct((s, n), jnp.bfloat16),
        scratch_shapes=[pltpu.VMEM((tm, d), jnp.bfloat16)],
        compiler_params=_cparams(("parallel", "arbitrary")),
        name="inproj",
    )(x, n1, sc1, sh1, w_in, qw, kw, cos, sin_e, sin_o)


SUBLANES = 8
GQA_KEY_CHUNK = 512
GQA_ONES_ROWS = 16
GQA_STEPS_PER_ITER = 2


def _sublane_allreduce(x, op):
    shift = SUBLANES // 2
    while shift:
        x = op(x, pltpu.roll(x, shift, 0))
        shift //= 2
    return x


def _gqa_kernel(q_ref, k_ref, vt_ref, o_ref, qt_ref, s0_ref, s1_ref, p0_ref, p1_ref, x0_ref, x1_ref,
                a0_ref, a1_ref, m_ref, acc_ref, *, tq, tk, n_chunks):
    s_refs, p_refs, x_refs, a_refs = (s0_ref, s1_ref), (p0_ref, p1_ref), (x0_ref, x1_ref), (a0_ref, a1_ref)
    nq = KV_GROUP * tq
    acc_rows = HEAD_DIM + GQA_ONES_ROWS
    for g in range(KV_GROUP):
        qg = q_ref[:, g * HEAD_DIM:(g + 1) * HEAD_DIM].astype(jnp.float32)
        qt_ref[:, g * tq:(g + 1) * tq] = qg.T.astype(qt_ref.dtype)
    m_ref[...] = jnp.full(m_ref.shape, -jnp.inf, jnp.float32)
    acc_ref[...] = jnp.zeros(acc_ref.shape, jnp.float32)

    def scores(c, slot):
        k = k_ref[pl.ds(pl.multiple_of(c * tk, tk), tk), :]
        s = jnp.dot(k, qt_ref[...], preferred_element_type=jnp.float32)
        s_refs[slot][...] = s
        x_refs[slot][...] = jnp.max(s.reshape(tk // SUBLANES, SUBLANES, nq), axis=0)

    def softmax(slot):
        m_old = m_ref[...]
        m_new = jnp.maximum(m_old, _sublane_allreduce(x_refs[slot][...], jnp.maximum))
        a_refs[slot][...] = jnp.exp2(m_old - m_new)
        m_ref[...] = m_new
        s = s_refs[slot][...].reshape(tk // SUBLANES, SUBLANES, nq)
        p_refs[slot][...] = jnp.exp2(s - m_new[None]).reshape(tk, nq).astype(p_refs[slot].dtype)

    def weighted_values(c, slot):
        acc = acc_ref[...].reshape(acc_rows // SUBLANES, SUBLANES, nq) * a_refs[slot][...][None]
        acc_ref[...] = acc.reshape(acc_rows, nq) + jnp.dot(
            vt_ref[c], p_refs[slot][...], preferred_element_type=jnp.float32)

    unroll = GQA_STEPS_PER_ITER
    assert unroll % 2 == 0 and (n_chunks - 2) % unroll == 0
    scores(0, 0)
    scores(1, 1)
    softmax(0)

    def body(i, carry):
        for u in range(unroll):
            t = unroll * i + 1 + u
            cur = (1 + u) % 2
            scores(t + 1, 1 - cur)
            softmax(cur)
            weighted_values(t - 1, 1 - cur)
        return carry

    lax.fori_loop(0, (n_chunks - 2) // unroll, body, 0)
    softmax(1)
    weighted_values(n_chunks - 2, 0)
    weighted_values(n_chunks - 1, 1)
    acc = acc_ref[...]
    inv_l = 1.0 / acc[HEAD_DIM:HEAD_DIM + SUBLANES, :]
    out_t = (acc[:HEAD_DIM, :].reshape(HEAD_DIM // SUBLANES, SUBLANES, nq) * inv_l[None]).reshape(HEAD_DIM, nq)
    for g in range(KV_GROUP):
        o_ref[:, g * HEAD_DIM:(g + 1) * HEAD_DIM] = out_t[:, g * tq:(g + 1) * tq].T.astype(o_ref.dtype)


def _gqa(proj, vt, *, n_q_heads, n_kv_heads, k_col, tq=256):
    s = proj.shape[0]
    _, n_chunks, acc_rows, tk = vt.shape
    gw = KV_GROUP * HEAD_DIM
    nq = KV_GROUP * tq
    kern = functools.partial(_gqa_kernel, tq=tq, tk=tk, n_chunks=n_chunks)
    stat = pltpu.VMEM((SUBLANES, nq), jnp.float32)
    return pl.pallas_call(
        kern,
        grid=(n_kv_heads, s // tq),
        in_specs=[pl.BlockSpec((tq, gw), lambda h, i: (i, h)),
                  pl.BlockSpec((s, HEAD_DIM), lambda h, i: (0, k_col // HEAD_DIM + h)),
                  pl.BlockSpec((None, n_chunks, acc_rows, tk), lambda h, i: (h, 0, 0, 0))],
        out_specs=pl.BlockSpec((tq, gw), lambda h, i: (i, h)),
        out_shape=jax.ShapeDtypeStruct((s, n_q_heads * HEAD_DIM), jnp.bfloat16),
        scratch_shapes=[pltpu.VMEM((HEAD_DIM, nq), jnp.bfloat16),
                        pltpu.VMEM((tk, nq), jnp.float32), pltpu.VMEM((tk, nq), jnp.float32),
                        pltpu.VMEM((tk, nq), jnp.bfloat16), pltpu.VMEM((tk, nq), jnp.bfloat16),
                        stat, stat, stat, stat, stat,
                        pltpu.VMEM((acc_rows, nq), jnp.float32)],
        compiler_params=_cparams(("parallel", "arbitrary")),
        name="gqa",
    )(proj, proj, vt)


def _chunk_transposed_values(proj, v_col, n_kv_heads, tk):
    s = proj.shape[0]
    v = proj[:, v_col:v_col + n_kv_heads * HEAD_DIM].reshape(s // tk, tk, n_kv_heads, HEAD_DIM)
    vt = v.transpose(2, 0, 3, 1)
    ones = jnp.ones(vt.shape[:2] + (GQA_ONES_ROWS, tk), vt.dtype)
    return jnp.concatenate([vt, ones], axis=2)


NA_GROUP = 4
NA_WIN_ROWS = NA_GROUP + NA_ROWS


def _na_window_start(r0, n_rows):
    return jnp.clip(r0 - NA_ROWS // 2, 0, n_rows - NA_WIN_ROWS)


def _natten_kernel(q_ref, k_ref, v_ref, b_ref, o_ref, *, groups_per_step, n_rows):
    gq = NA_GROUP * GRID_W
    win = NA_WIN_ROWS * GRID_W
    for g in range(groups_per_step):
        r0 = (pl.program_id(1) * groups_per_step + g) * NA_GROUP
        variant = jnp.where(r0 == 0, 0, jnp.where(r0 == n_rows - NA_GROUP, 2, 1))
        k0 = pl.multiple_of(_na_window_start(r0, n_rows) * GRID_W, GRID_W)
        q = q_ref[g * gq:(g + 1) * gq, :]
        kw = k_ref[pl.ds(k0, win), :]
        vw = v_ref[pl.ds(k0, win), :]
        s = lax.dot_general(q, kw, _NT, preferred_element_type=jnp.float32) + b_ref[variant]
        p = jnp.exp2(s - jnp.max(s, axis=-1, keepdims=True))
        l = jnp.sum(p, axis=-1, keepdims=True)
        o = jnp.dot(p.astype(vw.dtype), vw, preferred_element_type=jnp.float32) / l
        o_ref[g * gq:(g + 1) * gq, :] = o.astype(o_ref.dtype)


def _natten(proj, bias, *, n_heads, q_col, k_col, v_col, groups_per_step=4):
    s = proj.shape[0]
    n_rows = s // GRID_W
    tq = groups_per_step * NA_GROUP * GRID_W
    assert n_rows % (groups_per_step * NA_GROUP) == 0 and n_rows >= 2 * NA_WIN_ROWS
    kern = functools.partial(_natten_kernel, groups_per_step=groups_per_step, n_rows=n_rows)
    return pl.pallas_call(
        kern,
        grid=(n_heads, s // tq),
        in_specs=[pl.BlockSpec((tq, HEAD_DIM), lambda h, i: (i, q_col // HEAD_DIM + h)),
                  pl.BlockSpec((s, HEAD_DIM), lambda h, i: (0, k_col // HEAD_DIM + h)),
                  pl.BlockSpec((s, HEAD_DIM), lambda h, i: (0, v_col // HEAD_DIM + h)),
                  pl.BlockSpec((None,) + bias.shape[1:], lambda h, i: (h, 0, 0, 0))],
        out_specs=pl.BlockSpec((tq, HEAD_DIM), lambda h, i: (i, h)),
        out_shape=jax.ShapeDtypeStruct((s, n_heads * HEAD_DIM), jnp.bfloat16),
        compiler_params=_cparams(("parallel", "arbitrary")),
        name="natten",
    )(proj, proj, proj, bias)


def _natten_bias(rpb, n_rows):
    c = np.arange(GRID_W)[:, None]
    kc = np.arange(GRID_W)[None, :]
    cs = np.clip(c - NA_COLS // 2, 0, GRID_W - NA_COLS)
    col_ok = (kc >= cs) & (kc < cs + NA_COLS)
    n_heads, n_rel_r, n_rel_c = rpb.shape
    period = 2 * GRID_W
    assert n_rel_c <= period - GRID_W
    padded = jnp.pad(rpb, ((0, 0), (0, 0), (0, period - n_rel_c)))
    skew = jnp.tile(padded, (1, 1, GRID_W))[:, :, :GRID_W * (period - 1)]
    skew = skew.reshape(n_heads, n_rel_r, GRID_W, period - 1)
    toe = skew[:, :, :, NA_COLS - 1:NA_COLS - 1 + GRID_W] * LOG2E
    toe = jnp.where(col_ok[None, None], toe, NEG_BIG)
    masked = jnp.full((n_heads, GRID_W, GRID_W), NEG_BIG, toe.dtype)
    variants = []
    for r0 in (0, NA_WIN_ROWS, n_rows - NA_GROUP):
        ws = int(np.clip(r0 - NA_ROWS // 2, 0, n_rows - NA_WIN_ROWS))
        groups = []
        for i in range(NA_GROUP):
            rs = int(np.clip(r0 + i - NA_ROWS // 2, 0, n_rows - NA_ROWS))
            blocks = [toe[:, ws + wr - (r0 + i) + NA_ROWS - 1] if rs <= ws + wr < rs + NA_ROWS else masked
                      for wr in range(NA_WIN_ROWS)]
            groups.append(jnp.stack(blocks, axis=2))
        variants.append(jnp.stack(groups, axis=1))
    t = jnp.stack(variants, axis=1)
    return t.reshape(n_heads, 3, NA_GROUP * GRID_W, NA_WIN_ROWS * GRID_W)


def _merge_kernel(ya_ref, yb_ref, ga_ref, gb_ref, woa_ref, wob_ref, wout_ref, x_ref, g1_ref, n2_ref, sc_ref,
                  sh_ref, x1_ref, h2_ref, m_ref, *, tn, n_tiles):
    n = pl.program_id(1)
    a = jnp.dot(ya_ref[...], woa_ref[...], preferred_element_type=jnp.float32)
    b = jnp.dot(yb_ref[...], wob_ref[...], preferred_element_type=jnp.float32)
    m = (jax.nn.sigmoid(ga_ref[...].astype(jnp.float32)) * a
         + jax.nn.sigmoid(gb_ref[...].astype(jnp.float32)) * b)
    col = pl.multiple_of(n * tn, tn)
    m_ref[:, pl.ds(col, tn)] = m.astype(m_ref.dtype)

    @pl.when(n == n_tiles - 1)
    def _():
        y = jnp.dot(m_ref[...], wout_ref[...], preferred_element_type=jnp.float32)
        x1 = x_ref[...] + g1_ref[...] * y
        x1_ref[...] = x1
        h2_ref[...] = _rms_modulate(x1, n2_ref[...], sc_ref[...], sh_ref[...]).astype(h2_ref.dtype)


def _merge(ya, yb, proj, w_oa, w_ob, w_out, x, g1, n2, sc2, sh2, *, ga_col, gb_col, tm=512, tn=512):
    s, d = x.shape
    da = ya.shape[1]
    n_tiles = d // tn
    kern = functools.partial(_merge_kernel, tn=tn, n_tiles=n_tiles)
    row = lambda i, n: (0, 0)
    return pl.pallas_call(
        kern,
        grid=(s // tm, n_tiles),
        in_specs=[pl.BlockSpec((tm, da), lambda i, n: (i, 0)),
                  pl.BlockSpec((tm, da), lambda i, n: (i, 0)),
                  pl.BlockSpec((tm, tn), lambda i, n: (i, ga_col // tn + n)),
                  pl.BlockSpec((tm, tn), lambda i, n: (i, gb_col // tn + n)),
                  pl.BlockSpec((da, tn), lambda i, n: (0, n)),
                  pl.BlockSpec((da, tn), lambda i, n: (0, n)),
                  pl.BlockSpec((d, d), row, pipeline_mode=pl.Buffered(1)),
                  pl.BlockSpec((tm, d), lambda i, n: (i, 0)),
                  pl.BlockSpec((1, d), row), pl.BlockSpec((1, d), row),
                  pl.BlockSpec((1, d), row), pl.BlockSpec((1, d), row)],
        out_specs=[pl.BlockSpec((tm, d), lambda i, n: (i, 0)),
                   pl.BlockSpec((tm, d), lambda i, n: (i, 0))],
        out_shape=[jax.ShapeDtypeStruct((s, d), jnp.float32),
                   jax.ShapeDtypeStruct((s, d), jnp.bfloat16)],
        scratch_shapes=[pltpu.VMEM((tm, d), jnp.bfloat16)],
        compiler_params=_cparams(("parallel", "arbitrary")),
        name="merge",
    )(ya, yb, proj, proj, w_oa, w_ob, w_out, x, g1, n2, sc2, sh2)


def _ffn_kernel(h_ref, wg_ref, wu_ref, wd_ref, x1_ref, g2_ref, fw_ref, o_ref, acc_ref, *, n_tiles, final_norm):
    f = pl.program_id(1)

    @pl.when(f == 0)
    def _():
        acc_ref[...] = jnp.zeros(acc_ref.shape, jnp.float32)

    h = h_ref[...]
    g = jnp.dot(h, wg_ref[...], preferred_element_type=jnp.float32)
    u = jnp.dot(h, wu_ref[...], preferred_element_type=jnp.float32)
    a = (g * jax.nn.sigmoid(g) * u).astype(h.dtype)
    acc_ref[...] += jnp.dot(a, wd_ref[...], preferred_element_type=jnp.float32)

    @pl.when(f == n_tiles - 1)
    def _():
        x2 = x1_ref[...] + g2_ref[...] * acc_ref[...]
        if final_norm:
            x2 = x2 * lax.rsqrt(jnp.mean(x2 * x2, axis=-1, keepdims=True) + NORM_EPS) * fw_ref[...]
        o_ref[...] = x2


def _ffn(h2, w_gate, w_up, w_down, x1, g2, final_w, *, final_norm, tm=512, tf=512):
    s, d = x1.shape
    dff = w_gate.shape[1]
    n_tiles = dff // tf
    kern = functools.partial(_ffn_kernel, n_tiles=n_tiles, final_norm=final_norm)
    row = lambda i, f: (0, 0)
    return pl.pallas_call(
        kern,
        grid=(s // tm, n_tiles),
        in_specs=[pl.BlockSpec((tm, d), lambda i, f: (i, 0)),
                  pl.BlockSpec((d, tf), lambda i, f: (0, f)),
                  pl.BlockSpec((d, tf), lambda i, f: (0, f)),
                  pl.BlockSpec((tf, d), lambda i, f: (f, 0)),
                  pl.BlockSpec((tm, d), lambda i, f: (i, 0)),
                  pl.BlockSpec((1, d), row), pl.BlockSpec((1, d), row)],
        out_specs=pl.BlockSpec((tm, d), lambda i, f: (i, 0)),
        out_shape=jax.ShapeDtypeStruct((s, d), jnp.float32),
        scratch_shapes=[pltpu.VMEM((tm, d), jnp.float32)],
        compiler_params=_cparams(("parallel", "arbitrary")),
        name="ffn",
    )(h2, w_gate, w_up, w_down, x1, g2, final_w)


def _rope_tables(s):
    n_rows = s // GRID_W
    axis_dim = HEAD_DIM // 2
    inv = ROPE_THETA ** (-jnp.arange(0, axis_dim, 2, dtype=jnp.float32) / axis_dim)
    ang_r = jnp.arange(n_rows, dtype=jnp.float32)[:, None] * inv[None]
    ang_c = jnp.arange(GRID_W, dtype=jnp.float32)[:, None] * inv[None]
    even = (jnp.arange(HEAD_DIM) % 2 == 0)[None, :]

    def per_token(fn):
        shape = (n_rows, GRID_W, axis_dim // 2)
        pairs = jnp.concatenate([jnp.broadcast_to(fn(ang_r)[:, None, :], shape),
                                 jnp.broadcast_to(fn(ang_c)[None, :, :], shape)], axis=-1)
        return jnp.repeat(pairs.reshape(s, axis_dim), 2, axis=-1)

    cos, sin = per_token(jnp.cos), per_token(jnp.sin)
    return cos, jnp.where(even, -sin, 0.0), jnp.where(even, 0.0, sin)


def kernel(x, c, w_ada, b_ada, norm1_w, w_in, q_norm_w, k_norm_w, nat_rpb, w_oa, w_ob, w_out, norm2_w,
           w_ffn_gate, w_ffn_up, w_ffn_down, final_w):
    b, s, d = x.shape
    assert b == 1
    depth = w_ada.shape[0]
    n_heads_a = w_oa.shape[1] // HEAD_DIM
    n_heads_b = w_ob.shape[1] // HEAD_DIM
    n_kv_a = n_heads_a // KV_GROUP
    splits = (n_heads_a * HEAD_DIM, n_kv_a * HEAD_DIM, n_kv_a * HEAD_DIM,
              n_heads_b * HEAD_DIM, n_heads_b * HEAD_DIM, n_heads_b * HEAD_DIM, d, d)
    offs = [0] + np.cumsum(splits)[:-1].tolist()
    qa_col, ka_col, va_col, qb_col, kb_col, vb_col, ga_col, gb_col = offs
    assert qa_col == 0

    bf = jnp.bfloat16
    cos, sin_e, sin_o = _rope_tables(s)
    xs = x[0]
    c_col = c.reshape(d, 1)
    for l in range(depth):
        mod = _adaln(c_col, w_ada[l], b_ada[l][None, :])
        sh1, sc1, g1, sh2, sc2, g2 = [mod[:, i * d:(i + 1) * d] for i in range(6)]
        proj = _inproj(xs, norm1_w[l][None, :], sc1, sh1, w_in[l].astype(bf),
                       q_norm_w[l][None, :], k_norm_w[l][None, :], cos, sin_e, sin_o,
                       n_q_heads=n_heads_a, n_k_heads=n_kv_a, qb_cols=(qb_col, kb_col),
                       q_scale=HEAD_DIM ** -0.5 * LOG2E)
        ya = _gqa(proj, _chunk_transposed_values(proj, va_col, n_kv_a, GQA_KEY_CHUNK),
                  n_q_heads=n_heads_a, n_kv_heads=n_kv_a, k_col=ka_col)
        yb = _natten(proj, _natten_bias(nat_rpb[l], s // GRID_W), n_heads=n_heads_b,
                     q_col=qb_col, k_col=kb_col, v_col=vb_col)
        xs, h2 = _merge(ya, yb, proj, w_oa[l].astype(bf), w_ob[l].astype(bf), w_out[l].astype(bf),
                        xs, g1, norm2_w[l][None, :], sc2, sh2, ga_col=ga_col, gb_col=gb_col)
        xs = _ffn(h2, w_ffn_gate[l].astype(bf), w_ffn_up[l].astype(bf), w_ffn_down[l].astype(bf),
                  xs, g2, final_w[None, :], final_norm=(l == depth - 1))
    return xs[None]
```

```python
import functools
import math

import jax
import jax.numpy as jnp
import numpy as np
from jax import lax
from jax.experimental import pallas as pl
from jax.experimental.pallas import tpu as pltpu

HEAD_DIM = 128
KV_GROUP = 4
GRID_W = 64
NA_ROWS = 8
NA_COLS = 16
ROPE_THETA = 10000.0
NORM_EPS = 1e-6
NEG_BIG = -1e30
LOG2E = 1.4426950408889634

VMEM_LIMIT = 56 * 1024 * 1024

_NT = (((1,), (1,)), ((), ()))


def _cparams(sem):
    return pltpu.CompilerParams(dimension_semantics=sem, vmem_limit_bytes=VMEM_LIMIT)


def _adaln_kernel(c_ref, w_ref, b_ref, o_ref):
    c = c_ref[...]
    act = c * jax.nn.sigmoid(c)
    o_ref[...] = jnp.sum(act * w_ref[...], axis=0, keepdims=True) + b_ref[...]


def _adaln(c_col, w_ada, b_ada, tn=1024):
    d, n = w_ada.shape
    return pl.pallas_call(
        _adaln_kernel,
        grid=(n // tn,),
        in_specs=[pl.BlockSpec((d, 1), lambda j: (0, 0)),
                  pl.BlockSpec((d, tn), lambda j: (0, j)),
                  pl.BlockSpec((1, tn), lambda j: (0, j))],
        out_specs=pl.BlockSpec((1, tn), lambda j: (0, j)),
        out_shape=jax.ShapeDtypeStruct((1, n), jnp.float32),
        compiler_params=_cparams(("arbitrary",)),
        name="adaln",
    )(c_col, w_ada, b_ada)


def _rms_modulate(x, w, sc, sh):
    y = x * lax.rsqrt(jnp.mean(x * x, axis=-1, keepdims=True) + NORM_EPS)
    return (y * w) * (1.0 + sc) + sh


def _head_norm_rope(t, w, cos, sin_e, sin_o, scale):
    t = t * lax.rsqrt(jnp.mean(t * t, axis=-1, keepdims=True) + NORM_EPS) * w
    nxt = pltpu.roll(t, HEAD_DIM - 1, 1)
    prv = pltpu.roll(t, 1, 1)
    return (t * cos + nxt * sin_e + prv * sin_o) * scale


def _inproj_kernel(x_ref, n1_ref, sc_ref, sh_ref, w_ref, qw_ref, kw_ref, cos_ref, se_ref, so_ref,
                   o_ref, h_ref, *, n_q_tiles, heads_per_tile, n_k_heads, qb_tiles, q_scale):
    j = pl.program_id(1)

    @pl.when(j == 0)
    def _():
        h_ref[...] = _rms_modulate(x_ref[...], n1_ref[...], sc_ref[...], sh_ref[...]).astype(h_ref.dtype)

    acc = jnp.dot(h_ref[...], w_ref[...], preferred_element_type=jnp.float32)

    def rope_heads(n_heads, w, scale):
        cos, se, so = cos_ref[...], se_ref[...], so_ref[...]
        for hh in range(heads_per_tile):
            sl = slice(hh * HEAD_DIM, (hh + 1) * HEAD_DIM)
            t = acc[:, sl]
            if hh < n_heads:
                t = _head_norm_rope(t, w, cos, se, so, scale)
            o_ref[:, sl] = t.astype(o_ref.dtype)

    @pl.when(j < n_q_tiles)
    def _():
        rope_heads(heads_per_tile, qw_ref[...], q_scale)

    @pl.when(j == n_q_tiles)
    def _():
        rope_heads(n_k_heads, kw_ref[...], 1.0)

    in_qb = (j >= qb_tiles[0]) & (j < qb_tiles[1])

    @pl.when(in_qb)
    def _():
        o_ref[...] = (acc * q_scale).astype(o_ref.dtype)

    @pl.when((j > n_q_tiles) & jnp.logical_not(in_qb))
    def _():
        o_ref[...] = acc.astype(o_ref.dtype)


def _inproj(x, n1, sc1, sh1, w_in, qw, kw, cos, sin_e, sin_o, *, n_q_heads, n_k_heads, qb_cols, q_scale,
            tm=1024, tn=512):
    s, d = x.shape
    n = w_in.shape[1]
    heads_per_tile = tn // HEAD_DIM
    assert n_q_heads % heads_per_tile == 0 and n_k_heads <= heads_per_tile
    assert qb_cols[0] % tn == 0 and qb_cols[1] % tn == 0 and qb_cols[0] // tn > n_q_heads // heads_per_tile
    kern = functools.partial(_inproj_kernel, n_q_tiles=n_q_heads // heads_per_tile,
                             heads_per_tile=heads_per_tile, n_k_heads=n_k_heads,
                             qb_tiles=(qb_cols[0] // tn, qb_cols[1] // tn), q_scale=q_scale)
    row = lambda i, j: (0, 0)
    return pl.pallas_call(
        kern,
        grid=(s // tm, n // tn),
        in_specs=[pl.BlockSpec((tm, d), lambda i, j: (i, 0)),
                  pl.BlockSpec((1, d), row), pl.BlockSpec((1, d), row), pl.BlockSpec((1, d), row),
                  pl.BlockSpec((d, tn), lambda i, j: (0, j)),
                  pl.BlockSpec((1, HEAD_DIM), row), pl.BlockSpec((1, HEAD_DIM), row),
                  pl.BlockSpec((tm, HEAD_DIM), lambda i, j: (i, 0)),
                  pl.BlockSpec((tm, HEAD_DIM), lambda i, j: (i, 0)),
                  pl.BlockSpec((tm, HEAD_DIM), lambda i, j: (i, 0))],
        out_specs=pl.BlockSpec((tm, tn), lambda i, j: (i, j)),
        out_shape=jax.ShapeDtypeStruct((s, n), jnp.bfloat16),
        scratch_shapes=[pltpu.VMEM((tm, d), jnp.bfloat16)],
        compiler_params=_cparams(("parallel", "arbitrary")),
        name="inproj",
    )(x, n1, sc1, sh1, w_in, qw, kw, cos, sin_e, sin_o)


SUBLANES = 8
GQA_KEY_CHUNK = 512
GQA_ONES_ROWS = 16
GQA_STEPS_PER_ITER = 2


def _sublane_allreduce(x, op):
    shift = SUBLANES // 2
    while shift:
        x = op(x, pltpu.roll(x, shift, 0))
        shift //= 2
    return x


def _gqa_kernel(q_ref, k_ref, vt_ref, o_ref, qt_ref, s0_ref, s1_ref, p0_ref, p1_ref, x0_ref, x1_ref,
                a0_ref, a1_ref, m_ref, acc_ref, *, tq, tk, n_chunks):
    s_refs, p_refs, x_refs, a_refs = (s0_ref, s1_ref), (p0_ref, p1_ref), (x0_ref, x1_ref), (a0_ref, a1_ref)
    nq = KV_GROUP * tq
    acc_rows = HEAD_DIM + GQA_ONES_ROWS
    for g in range(KV_GROUP):
        qg = q_ref[:, g * HEAD_DIM:(g + 1) * HEAD_DIM].astype(jnp.float32)
        qt_ref[:, g * tq:(g + 1) * tq] = qg.T.astype(qt_ref.dtype)
    m_ref[...] = jnp.full(m_ref.shape, -jnp.inf, jnp.float32)
    acc_ref[...] = jnp.zeros(acc_ref.shape, jnp.float32)

    def scores(c, slot):
        k = k_ref[pl.ds(pl.multiple_of(c * tk, tk), tk), :]
        s = jnp.dot(k, qt_ref[...], preferred_element_type=jnp.float32)
        s_refs[slot][...] = s
        x_refs[slot][...] = jnp.max(s.reshape(tk // SUBLANES, SUBLANES, nq), axis=0)

    def softmax(slot):
        m_old = m_ref[...]
        m_new = jnp.maximum(m_old, _sublane_allreduce(x_refs[slot][...], jnp.maximum))
        a_refs[slot][...] = jnp.exp2(m_old - m_new)
        m_ref[...] = m_new
        s = s_refs[slot][...].reshape(tk // SUBLANES, SUBLANES, nq)
        p_refs[slot][...] = jnp.exp2(s - m_new[None]).reshape(tk, nq).astype(p_refs[slot].dtype)

    def weighted_values(c, slot):
        acc = acc_ref[...].reshape(acc_rows // SUBLANES, SUBLANES, nq) * a_refs[slot][...][None]
        acc_ref[...] = acc.reshape(acc_rows, nq) + jnp.dot(
            vt_ref[c], p_refs[slot][...], preferred_element_type=jnp.float32)

    unroll = GQA_STEPS_PER_ITER
    assert unroll % 2 == 0 and (n_chunks - 2) % unroll == 0
    scores(0, 0)
    scores(1, 1)
    softmax(0)

    def body(i, carry):
        for u in range(unroll):
            t = unroll * i + 1 + u
            cur = (1 + u) % 2
            scores(t + 1, 1 - cur)
            softmax(cur)
            weighted_values(t - 1, 1 - cur)
        return carry

    lax.fori_loop(0, (n_chunks - 2) // unroll, body, 0)
    softmax(1)
    weighted_values(n_chunks - 2, 0)
    weighted_values(n_chunks - 1, 1)
    acc = acc_ref[...]
    inv_l = 1.0 / acc[HEAD_DIM:HEAD_DIM + SUBLANES, :]
    out_t = (acc[:HEAD_DIM, :].reshape(HEAD_DIM // SUBLANES, SUBLANES, nq) * inv_l[None]).reshape(HEAD_DIM, nq)
    for g in range(KV_GROUP):
        o_ref[:, g * HEAD_DIM:(g + 1) * HEAD_DIM] = out_t[:, g * tq:(g + 1) * tq].T.astype(o_ref.dtype)


def _gqa(proj, vt, *, n_q_heads, n_kv_heads, k_col, tq=1024):
    s = proj.shape[0]
    _, n_chunks, acc_rows, tk = vt.shape
    gw = KV_GROUP * HEAD_DIM
    nq = KV_GROUP * tq
    kern = functools.partial(_gqa_kernel, tq=tq, tk=tk, n_chunks=n_chunks)
    stat = pltpu.VMEM((SUBLANES, nq), jnp.float32)
    return pl.pallas_call(
        kern,
        grid=(n_kv_heads, s // tq),
        in_specs=[pl.BlockSpec((tq, gw), lambda h, i: (i, h)),
                  pl.BlockSpec((s, HEAD_DIM), lambda h, i: (0, k_col // HEAD_DIM + h),
                               pipeline_mode=pl.Buffered(1)),
                  pl.BlockSpec((None, n_chunks, acc_rows, tk), lambda h, i: (h, 0, 0, 0),
                               pipeline_mode=pl.Buffered(1))],
        out_specs=pl.BlockSpec((tq, gw), lambda h, i: (i, h)),
        out_shape=jax.ShapeDtypeStruct((s, n_q_heads * HEAD_DIM), jnp.bfloat16),
        scratch_shapes=[pltpu.VMEM((HEAD_DIM, nq), jnp.bfloat16),
                        pltpu.VMEM((tk, nq), jnp.float32), pltpu.VMEM((tk, nq), jnp.float32),
                        pltpu.VMEM((tk, nq), jnp.bfloat16), pltpu.VMEM((tk, nq), jnp.bfloat16),
                        stat, stat, stat, stat, stat,
                        pltpu.VMEM((acc_rows, nq), jnp.float32)],
        compiler_params=_cparams(("parallel", "arbitrary")),
        name="gqa",
    )(proj, proj, vt)


def _chunk_transposed_values(proj, v_col, n_kv_heads, tk):
    s = proj.shape[0]
    v = proj[:, v_col:v_col + n_kv_heads * HEAD_DIM].reshape(s // tk, tk, n_kv_heads, HEAD_DIM)
    vt = v.transpose(2, 0, 3, 1)
    ones = jnp.ones(vt.shape[:2] + (GQA_ONES_ROWS, tk), vt.dtype)
    return jnp.concatenate([vt, ones], axis=2)


NA_GROUP = 4
NA_WIN_ROWS = NA_GROUP + NA_ROWS


def _na_window_start(r0, n_rows):
    return jnp.clip(r0 - NA_ROWS // 2, 0, n_rows - NA_WIN_ROWS)


def _na_bias_rows(n_rows):
    rel = []
    for r0 in (0, NA_WIN_ROWS, n_rows - NA_GROUP):
        ws = int(np.clip(r0 - NA_ROWS // 2, 0, n_rows - NA_WIN_ROWS))
        per_row = []
        for i in range(NA_GROUP):
            rs = int(np.clip(r0 + i - NA_ROWS // 2, 0, n_rows - NA_ROWS))
            per_row.append([ws + wr - (r0 + i) + NA_ROWS - 1 if rs <= ws + wr < rs + NA_ROWS else None
                            for wr in range(NA_WIN_ROWS)])
        rel.append(per_row)
    return rel


def _na_build_bias(rpb_ref, b_ref, n_rows):
    shape = (GRID_W, 2 * GRID_W)
    lane = lax.broadcasted_iota(jnp.int32, shape, 1)
    c = lax.broadcasted_iota(jnp.int32, shape, 0)
    left = lane < GRID_W
    kc = jnp.where(left, lane, lane - GRID_W)
    cs = jnp.clip(c - NA_COLS // 2, 0, GRID_W - NA_COLS)
    col_ok = (kc >= cs) & (kc < cs + NA_COLS)
    neg = jnp.full(shape, NEG_BIG, jnp.float32)
    n_rel = 2 * NA_ROWS - 1
    tiles = []
    for dr in range(n_rel):
        row = jnp.broadcast_to(rpb_ref[dr:dr + 1, :], shape) * LOG2E
        lo = pltpu.roll(row, 2 * GRID_W - (NA_COLS - 1), 1, stride=1, stride_axis=0)
        hi = pltpu.roll(row, GRID_W - (NA_COLS - 1), 1, stride=1, stride_axis=0)
        tiles.append(jnp.where(col_ok, jnp.where(left, lo, hi), neg))
    rel = _na_bias_rows(n_rows)
    for v in range(3):
        for i in range(NA_GROUP):
            for j in range(NA_WIN_ROWS // 2):
                a, b = rel[v][i][2 * j], rel[v][i][2 * j + 1]
                pair = jnp.where(left, neg if a is None else tiles[a], neg if b is None else tiles[b])
                b_ref[v, i * GRID_W:(i + 1) * GRID_W, j * 2 * GRID_W:(j + 1) * 2 * GRID_W] = pair


def _natten_kernel(q_ref, k_ref, v_ref, rpb_ref, o_ref, b_ref, *, groups_per_step, n_rows):
    gq = NA_GROUP * GRID_W
    win = NA_WIN_ROWS * GRID_W

    @pl.when(pl.program_id(1) == 0)
    def _():
        _na_build_bias(rpb_ref, b_ref, n_rows)

    for g in range(groups_per_step):
        r0 = (pl.program_id(1) * groups_per_step + g) * NA_GROUP
        variant = jnp.where(r0 == 0, 0, jnp.where(r0 == n_rows - NA_GROUP, 2, 1))
        k0 = pl.multiple_of(_na_window_start(r0, n_rows) * GRID_W, GRID_W)
        q = q_ref[g * gq:(g + 1) * gq, :]
        kw = k_ref[pl.ds(k0, win), :]
        vw = v_ref[pl.ds(k0, win), :]
        s = lax.dot_general(q, kw, _NT, preferred_element_type=jnp.float32) + b_ref[variant]
        p = jnp.exp2(s - jnp.max(s, axis=-1, keepdims=True))
        l = jnp.sum(p, axis=-1, keepdims=True)
        o = jnp.dot(p.astype(vw.dtype), vw, preferred_element_type=jnp.float32) / l
        o_ref[g * gq:(g + 1) * gq, :] = o.astype(o_ref.dtype)


def _natten(proj, rpb, *, n_heads, q_col, k_col, v_col, groups_per_step=4):
    s = proj.shape[0]
    n_rows = s // GRID_W
    tq = groups_per_step * NA_GROUP * GRID_W
    gq, win = NA_GROUP * GRID_W, NA_WIN_ROWS * GRID_W
    assert n_rows % (groups_per_step * NA_GROUP) == 0 and n_rows >= 2 * NA_WIN_ROWS
    assert NA_WIN_ROWS % 2 == 0 and 2 * GRID_W == HEAD_DIM and rpb.shape[2] <= GRID_W
    rel_rows = -(-rpb.shape[1] // SUBLANES) * SUBLANES
    rpb_rows = jnp.pad(rpb, ((0, 0), (0, rel_rows - rpb.shape[1]), (0, 2 * GRID_W - rpb.shape[2])))
    kern = functools.partial(_natten_kernel, groups_per_step=groups_per_step, n_rows=n_rows)
    return pl.pallas_call(
        kern,
        grid=(n_heads, s // tq),
        in_specs=[pl.BlockSpec((tq, HEAD_DIM), lambda h, i: (i, q_col // HEAD_DIM + h)),
                  pl.BlockSpec((s, HEAD_DIM), lambda h, i: (0, k_col // HEAD_DIM + h)),
                  pl.BlockSpec((s, HEAD_DIM), lambda h, i: (0, v_col // HEAD_DIM + h)),
                  pl.BlockSpec((None, rel_rows, 2 * GRID_W), lambda h, i: (h, 0, 0))],
        out_specs=pl.BlockSpec((tq, HEAD_DIM), lambda h, i: (i, h)),
        out_shape=jax.ShapeDtypeStruct((s, n_heads * HEAD_DIM), jnp.bfloat16),
        scratch_shapes=[pltpu.VMEM((3, gq, win), jnp.float32)],
        compiler_params=_cparams(("arbitrary", "arbitrary")),
        name="natten",
    )(proj, proj, proj, rpb_rows)


def _merge_kernel(ya_ref, yb_ref, ga_ref, gb_ref, woa_ref, wob_ref, wout_ref, x_ref, g1_ref, n2_ref, sc_ref,
                  sh_ref, x1_ref, h2_ref, m_ref, *, tn, n_tiles):
    n = pl.program_id(1)
    a = jnp.dot(ya_ref[...], woa_ref[...], preferred_element_type=jnp.float32)
    b = jnp.dot(yb_ref[...], wob_ref[...], preferred_element_type=jnp.float32)
    m = (jax.nn.sigmoid(ga_ref[...].astype(jnp.float32)) * a
         + jax.nn.sigmoid(gb_ref[...].astype(jnp.float32)) * b)
    col = pl.multiple_of(n * tn, tn)
    m_ref[:, pl.ds(col, tn)] = m.astype(m_ref.dtype)

    @pl.when(n == n_tiles - 1)
    def _():
        y = jnp.dot(m_ref[...], wout_ref[...], preferred_element_type=jnp.float32)
        x1 = x_ref[...] + g1_ref[...] * y
        x1_ref[...] = x1
        h2_ref[...] = _rms_modulate(x1, n2_ref[...], sc_ref[...], sh_ref[...]).astype(h2_ref.dtype)


def _merge(ya, yb, proj, w_oa, w_ob, w_out, x, g1, n2, sc2, sh2, *, ga_col, gb_col, tm=512, tn=512):
    s, d = x.shape
    da = ya.shape[1]
    n_tiles = d // tn
    kern = functools.partial(_merge_kernel, tn=tn, n_tiles=n_tiles)
    row = lambda i, n: (0, 0)
    return pl.pallas_call(
        kern,
        grid=(s // tm, n_tiles),
        in_specs=[pl.BlockSpec((tm, da), lambda i, n: (i, 0)),
                  pl.BlockSpec((tm, da), lambda i, n: (i, 0)),
                  pl.BlockSpec((tm, tn), lambda i, n: (i, ga_col // tn + n)),
                  pl.BlockSpec((tm, tn), lambda i, n: (i, gb_col // tn + n)),
                  pl.BlockSpec((da, tn), lambda i, n: (0, n)),
                  pl.BlockSpec((da, tn), lambda i, n: (0, n)),
                  pl.BlockSpec((d, d), row, pipeline_mode=pl.Buffered(1)),
                  pl.BlockSpec((tm, d), lambda i, n: (i, 0)),
                  pl.BlockSpec((1, d), row), pl.BlockSpec((1, d), row),
                  pl.BlockSpec((1, d), row), pl.BlockSpec((1, d), row)],
        out_specs=[pl.BlockSpec((tm, d), lambda i, n: (i, 0)),
                   pl.BlockSpec((tm, d), lambda i, n: (i, 0))],
        out_shape=[jax.ShapeDtypeStruct((s, d), jnp.float32),
                   jax.ShapeDtypeStruct((s, d), jnp.bfloat16)],
        scratch_shapes=[pltpu.VMEM((tm, d), jnp.bfloat16)],
        compiler_params=_cparams(("parallel", "arbitrary")),
        name="merge",
    )(ya, yb, proj, proj, w_oa, w_ob, w_out, x, g1, n2, sc2, sh2)


def _ffn_kernel(h_ref, wg_ref, wu_ref, wd_ref, x1_ref, g2_ref, fw_ref, o_ref, acc_ref, *, n_tiles, final_norm):
    f = pl.program_id(1)

    @pl.when(f == 0)
    def _():
        acc_ref[...] = jnp.zeros(acc_ref.shape, jnp.float32)

    h = h_ref[...]
    g = jnp.dot(h, wg_ref[...], preferred_element_type=jnp.float32)
    u = jnp.dot(h, wu_ref[...], preferred_element_type=jnp.float32)
    a = (g * jax.nn.sigmoid(g) * u).astype(h.dtype)
    acc_ref[...] += jnp.dot(a, wd_ref[...], preferred_element_type=jnp.float32)

    @pl.when(f == n_tiles - 1)
    def _():
        x2 = x1_ref[...] + g2_ref[...] * acc_ref[...]
        if final_norm:
            x2 = x2 * lax.rsqrt(jnp.mean(x2 * x2, axis=-1, keepdims=True) + NORM_EPS) * fw_ref[...]
        o_ref[...] = x2


def _ffn(h2, w_gate, w_up, w_down, x1, g2, final_w, *, final_norm, tm=512, tf=512):
    s, d = x1.shape
    dff = w_gate.shape[1]
    n_tiles = dff // tf
    kern = functools.partial(_ffn_kernel, n_tiles=n_tiles, final_norm=final_norm)
    row = lambda i, f: (0, 0)
    return pl.pallas_call(
        kern,
        grid=(s // tm, n_tiles),
        in_specs=[pl.BlockSpec((tm, d), lambda i, f: (i, 0)),
                  pl.BlockSpec((d, tf), lambda i, f: (0, f)),
                  pl.BlockSpec((d, tf), lambda i, f: (0, f)),
                  pl.BlockSpec((tf, d), lambda i, f: (f, 0)),
                  pl.BlockSpec((tm, d), lambda i, f: (i, 0)),
                  pl.BlockSpec((1, d), row), pl.BlockSpec((1, d), row)],
        out_specs=pl.BlockSpec((tm, d), lambda i, f: (i, 0)),
        out_shape=jax.ShapeDtypeStruct((s, d), jnp.float32),
        scratch_shapes=[pltpu.VMEM((tm, d), jnp.float32)],
        compiler_params=_cparams(("parallel", "arbitrary")),
        name="ffn",
    )(h2, w_gate, w_up, w_down, x1, g2, final_w)


def _rope_tables(s):
    n_rows = s // GRID_W
    axis_dim = HEAD_DIM // 2
    inv = ROPE_THETA ** (-jnp.arange(0, axis_dim, 2, dtype=jnp.float32) / axis_dim)
    lane = np.arange(HEAD_DIM)
    inv_lane = inv[(lane // 2) % (axis_dim // 2)]
    by_col = (lane >= axis_dim)[None, None, :]
    even = (lane % 2 == 0)[None, :]
    ang_r = jnp.arange(n_rows, dtype=jnp.float32)[:, None] * inv_lane[None]
    ang_c = jnp.arange(GRID_W, dtype=jnp.float32)[:, None] * inv_lane[None]

    def per_token(fn):
        return jnp.where(by_col, fn(ang_c)[None, :, :], fn(ang_r)[:, None, :]).reshape(s, HEAD_DIM)

    cos, sin = per_token(jnp.cos), per_token(jnp.sin)
    return cos, jnp.where(even, -sin, 0.0), jnp.where(even, 0.0, sin)


def kernel(x, c, w_ada, b_ada, norm1_w, w_in, q_norm_w, k_norm_w, nat_rpb, w_oa, w_ob, w_out, norm2_w,
           w_ffn_gate, w_ffn_up, w_ffn_down, final_w):
    b, s, d = x.shape
    assert b == 1
    depth = w_ada.shape[0]
    n_heads_a = w_oa.shape[1] // HEAD_DIM
    n_heads_b = w_ob.shape[1] // HEAD_DIM
    n_kv_a = n_heads_a // KV_GROUP
    splits = (n_heads_a * HEAD_DIM, n_kv_a * HEAD_DIM, n_kv_a * HEAD_DIM,
              n_heads_b * HEAD_DIM, n_heads_b * HEAD_DIM, n_heads_b * HEAD_DIM, d, d)
    offs = [0] + np.cumsum(splits)[:-1].tolist()
    qa_col, ka_col, va_col, qb_col, kb_col, vb_col, ga_col, gb_col = offs
    assert qa_col == 0

    bf = jnp.bfloat16
    cos, sin_e, sin_o = _rope_tables(s)
    xs = x[0]
    c_col = c.reshape(d, 1)
    for l in range(depth):
        mod = _adaln(c_col, w_ada[l], b_ada[l][None, :])
        sh1, sc1, g1, sh2, sc2, g2 = [mod[:, i * d:(i + 1) * d] for i in range(6)]
        proj = _inproj(xs, norm1_w[l][None, :], sc1, sh1, w_in[l].astype(bf),
                       q_norm_w[l][None, :], k_norm_w[l][None, :], cos, sin_e, sin_o,
                       n_q_heads=n_heads_a, n_k_heads=n_kv_a, qb_cols=(qb_col, kb_col),
                       q_scale=HEAD_DIM ** -0.5 * LOG2E)
        ya = _gqa(proj, _chunk_transposed_values(proj, va_col, n_kv_a, GQA_KEY_CHUNK),
                  n_q_heads=n_heads_a, n_kv_heads=n_kv_a, k_col=ka_col)
        yb = _natten(proj, nat_rpb[l], n_heads=n_heads_b,
                     q_col=qb_col, k_col=kb_col, v_col=vb_col)
        xs, h2 = _merge(ya, yb, proj, w_oa[l].astype(bf), w_ob[l].astype(bf), w_out[l].astype(bf),
                        xs, g1, norm2_w[l][None, :], sc2, sh2, ga_col=ga_col, gb_col=gb_col)
        xs = _ffn(h2, w_ffn_gate[l].astype(bf), w_ffn_up[l].astype(bf), w_ffn_down[l].astype(bf),
                  xs, g2, final_w[None, :], final_norm=(l == depth - 1))
    return xs[None]
```

```python
import functools
import math

import jax
import jax.numpy as jnp
import numpy as np
from jax import lax
from jax.experimental import pallas as pl
from jax.experimental.pallas import tpu as pltpu

HEAD_DIM = 128
KV_GROUP = 4
GRID_W = 64
NA_ROWS = 8
NA_COLS = 16
ROPE_THETA = 10000.0
NORM_EPS = 1e-6
NEG_BIG = -1e30
LOG2E = 1.4426950408889634

VMEM_LIMIT = 56 * 1024 * 1024

_NT = (((1,), (1,)), ((), ()))


def _cparams(sem):
    return pltpu.CompilerParams(dimension_semantics=sem, vmem_limit_bytes=VMEM_LIMIT)


def _adaln_kernel(c_ref, w_ref, b_ref, o_ref):
    c = c_ref[...]
    act = c * jax.nn.sigmoid(c)
    o_ref[...] = jnp.sum(act * w_ref[...], axis=0, keepdims=True) + b_ref[...]


def _adaln(c_col, w_ada, b_ada, tn=1024):
    d, n = w_ada.shape
    return pl.pallas_call(
        _adaln_kernel,
        grid=(n // tn,),
        in_specs=[pl.BlockSpec((d, 1), lambda j: (0, 0)),
                  pl.BlockSpec((d, tn), lambda j: (0, j)),
                  pl.BlockSpec((1, tn), lambda j: (0, j))],
        out_specs=pl.BlockSpec((1, tn), lambda j: (0, j)),
        out_shape=jax.ShapeDtypeStruct((1, n), jnp.float32),
        compiler_params=_cparams(("arbitrary",)),
        name="adaln",
    )(c_col, w_ada, b_ada)


def _rms_modulate(x, w, sc, sh):
    rstd = lax.rsqrt(jnp.mean(x * x, axis=-1, keepdims=True) + NORM_EPS)
    return (x * rstd) * (w * (1.0 + sc)) + sh


def _head_norm_rope(t, w, cos, sin_e, sin_o, scale):
    t = t * lax.rsqrt(jnp.mean(t * t, axis=-1, keepdims=True) + NORM_EPS) * w
    nxt = pltpu.roll(t, HEAD_DIM - 1, 1)
    prv = pltpu.roll(t, 1, 1)
    return (t * cos + nxt * sin_e + prv * sin_o) * scale


def _inproj_kernel(x_ref, n1_ref, sc_ref, sh_ref, w_ref, qw_ref, kw_ref, cos_ref, se_ref, so_ref,
                   o_ref, h_ref, *, n_q_tiles, heads_per_tile, n_k_heads, qb_tiles, q_scale):
    j = pl.program_id(1)

    @pl.when(j == 0)
    def _():
        h_ref[...] = _rms_modulate(x_ref[...], n1_ref[...], sc_ref[...], sh_ref[...]).astype(h_ref.dtype)

    acc = jnp.dot(h_ref[...], w_ref[...], preferred_element_type=jnp.float32)

    def rope_heads(n_heads, w, scale):
        cos, se, so = cos_ref[...], se_ref[...], so_ref[...]
        for hh in range(heads_per_tile):
            sl = slice(hh * HEAD_DIM, (hh + 1) * HEAD_DIM)
            t = acc[:, sl]
            if hh < n_heads:
                t = _head_norm_rope(t, w, cos, se, so, scale)
            o_ref[:, sl] = t.astype(o_ref.dtype)

    @pl.when(j < n_q_tiles)
    def _():
        rope_heads(heads_per_tile, qw_ref[...], q_scale)

    @pl.when(j == n_q_tiles)
    def _():
        rope_heads(n_k_heads, kw_ref[...], 1.0)

    in_qb = (j >= qb_tiles[0]) & (j < qb_tiles[1])

    @pl.when(in_qb)
    def _():
        o_ref[...] = (acc * q_scale).astype(o_ref.dtype)

    @pl.when((j > n_q_tiles) & jnp.logical_not(in_qb))
    def _():
        o_ref[...] = acc.astype(o_ref.dtype)


def _inproj(x, n1, sc1, sh1, w_in, qw, kw, cos, sin_e, sin_o, *, n_q_heads, n_k_heads, qb_cols, q_scale,
            tm=1024, tn=512):
    s, d = x.shape
    n = w_in.shape[1]
    heads_per_tile = tn // HEAD_DIM
    assert n_q_heads % heads_per_tile == 0 and n_k_heads <= heads_per_tile
    assert qb_cols[0] % tn == 0 and qb_cols[1] % tn == 0 and qb_cols[0] // tn > n_q_heads // heads_per_tile
    kern = functools.partial(_inproj_kernel, n_q_tiles=n_q_heads // heads_per_tile,
                             heads_per_tile=heads_per_tile, n_k_heads=n_k_heads,
                             qb_tiles=(qb_cols[0] // tn, qb_cols[1] // tn), q_scale=q_scale)
    row = lambda i, j: (0, 0)
    return pl.pallas_call(
        kern,
        grid=(s // tm, n // tn),
        in_specs=[pl.BlockSpec((tm, d), lambda i, j: (i, 0)),
                  pl.BlockSpec((1, d), row), pl.BlockSpec((1, d), row), pl.BlockSpec((1, d), row),
                  pl.BlockSpec((d, tn), lambda i, j: (0, j)),
                  pl.BlockSpec((1, HEAD_DIM), row), pl.BlockSpec((1, HEAD_DIM), row),
                  pl.BlockSpec((tm, HEAD_DIM), lambda i, j: (i, 0)),
                  pl.BlockSpec((tm, HEAD_DIM), lambda i, j: (i, 0)),
                  pl.BlockSpec((tm, HEAD_DIM), lambda i, j: (i, 0))],
        out_specs=pl.BlockSpec((tm, tn), lambda i, j: (i, j)),
        out_shape=jax.ShapeDtypeStruct((s, n), jnp.bfloat16),
        scratch_shapes=[pltpu.VMEM((tm, d), jnp.bfloat16)],
        compiler_params=_cparams(("parallel", "arbitrary")),
        name="inproj",
    )(x, n1, sc1, sh1, w_in, qw, kw, cos, sin_e, sin_o)


SUBLANES = 8
GQA_KEY_CHUNK = 512
GQA_ONES_ROWS = 16
GQA_STEPS_PER_ITER = 2


def _sublane_allreduce(x, op):
    shift = SUBLANES // 2
    while shift:
        x = op(x, pltpu.roll(x, shift, 0))
        shift //= 2
    return x


def _gqa_kernel(q_ref, qn_ref, k_ref, vt_ref, o_ref, qt_ref, s0_ref, s1_ref, p0_ref, p1_ref, x0_ref, x1_ref,
                a0_ref, a1_ref, m_ref, acc_ref, *, tq, tk, n_chunks):
    s_refs, p_refs, x_refs, a_refs = (s0_ref, s1_ref), (p0_ref, p1_ref), (x0_ref, x1_ref), (a0_ref, a1_ref)
    nq = KV_GROUP * tq
    acc_rows = HEAD_DIM + GQA_ONES_ROWS

    def scores(c, slot):
        k = k_ref[pl.ds(pl.multiple_of(c * tk, tk), tk), :]
        s = jnp.dot(k, qt_ref[...], preferred_element_type=jnp.float32)
        s_refs[slot][...] = s
        x_refs[slot][...] = jnp.max(s.reshape(tk // SUBLANES, SUBLANES, nq), axis=0)

    def softmax(slot):
        m_old = m_ref[...]
        m_new = jnp.maximum(m_old, _sublane_allreduce(x_refs[slot][...], jnp.maximum))
        a_refs[slot][...] = jnp.exp2(m_old - m_new)
        m_ref[...] = m_new
        s = s_refs[slot][...].reshape(tk // SUBLANES, SUBLANES, nq)
        p_refs[slot][...] = jnp.exp2(s - m_new[None]).reshape(tk, nq).astype(p_refs[slot].dtype)

    def weighted_values(c, slot):
        acc = acc_ref[...].reshape(acc_rows // SUBLANES, SUBLANES, nq) * a_refs[slot][...][None]
        acc_ref[...] = acc.reshape(acc_rows, nq) + jnp.dot(
            vt_ref[c], p_refs[slot][...], preferred_element_type=jnp.float32)

    def load_queries(src_ref):
        for g in range(KV_GROUP):
            qg = src_ref[:, g * HEAD_DIM:(g + 1) * HEAD_DIM].astype(jnp.float32)
            qt_ref[:, g * tq:(g + 1) * tq] = qg.T.astype(qt_ref.dtype)

    def reset_and_first_softmax():
        m_ref[...] = jnp.full(m_ref.shape, -jnp.inf, jnp.float32)
        acc_ref[...] = jnp.zeros(acc_ref.shape, jnp.float32)
        softmax(0)

    unroll = GQA_STEPS_PER_ITER
    assert unroll % 2 == 0 and (n_chunks - 2) % unroll == 0

    @pl.when(pl.program_id(1) == 0)
    def _():
        load_queries(q_ref)
        scores(0, 0)
        scores(1, 1)
        reset_and_first_softmax()

    def body(i, carry):
        for u in range(unroll):
            t = unroll * i + 1 + u
            cur = (1 + u) % 2
            scores(t + 1, 1 - cur)
            softmax(cur)
            weighted_values(t - 1, 1 - cur)
        return carry

    lax.fori_loop(0, (n_chunks - 2) // unroll, body, 0)
    load_queries(qn_ref)
    scores(0, 0)
    softmax(1)
    weighted_values(n_chunks - 2, 0)
    scores(1, 1)
    weighted_values(n_chunks - 1, 1)
    acc = acc_ref[...]
    inv_l = 1.0 / acc[HEAD_DIM:HEAD_DIM + SUBLANES, :]
    out_t = (acc[:HEAD_DIM, :].reshape(HEAD_DIM // SUBLANES, SUBLANES, nq) * inv_l[None]).reshape(HEAD_DIM, nq)
    for g in range(KV_GROUP):
        o_ref[:, g * HEAD_DIM:(g + 1) * HEAD_DIM] = out_t[:, g * tq:(g + 1) * tq].T.astype(o_ref.dtype)
    reset_and_first_softmax()


def _gqa(proj, vt, *, n_q_heads, n_kv_heads, k_col, tq=1024):
    s = proj.shape[0]
    _, n_chunks, acc_rows, tk = vt.shape
    gw = KV_GROUP * HEAD_DIM
    nq = KV_GROUP * tq
    n_blocks = s // tq
    kern = functools.partial(_gqa_kernel, tq=tq, tk=tk, n_chunks=n_chunks)
    stat = pltpu.VMEM((SUBLANES, nq), jnp.float32)
    return pl.pallas_call(
        kern,
        grid=(n_kv_heads, n_blocks),
        in_specs=[pl.BlockSpec((tq, gw), lambda h, i: (i, h)),
                  pl.BlockSpec((tq, gw), lambda h, i: (jnp.minimum(i + 1, n_blocks - 1), h)),
                  pl.BlockSpec((s, HEAD_DIM), lambda h, i: (0, k_col // HEAD_DIM + h),
                               pipeline_mode=pl.Buffered(1)),
                  pl.BlockSpec((None, n_chunks, acc_rows, tk), lambda h, i: (h, 0, 0, 0),
                               pipeline_mode=pl.Buffered(1))],
        out_specs=pl.BlockSpec((tq, gw), lambda h, i: (i, h)),
        out_shape=jax.ShapeDtypeStruct((s, n_q_heads * HEAD_DIM), jnp.bfloat16),
        scratch_shapes=[pltpu.VMEM((HEAD_DIM, nq), jnp.bfloat16),
                        pltpu.VMEM((tk, nq), jnp.float32), pltpu.VMEM((tk, nq), jnp.float32),
                        pltpu.VMEM((tk, nq), jnp.bfloat16), pltpu.VMEM((tk, nq), jnp.bfloat16),
                        stat, stat, stat, stat, stat,
                        pltpu.VMEM((acc_rows, nq), jnp.float32)],
        compiler_params=_cparams(("arbitrary", "arbitrary")),
        name="gqa",
    )(proj, proj, proj, vt)


def _chunk_transposed_values(proj, v_col, n_kv_heads, tk):
    s = proj.shape[0]
    v = proj[:, v_col:v_col + n_kv_heads * HEAD_DIM].reshape(s // tk, tk, n_kv_heads, HEAD_DIM)
    vt = v.transpose(2, 0, 3, 1)
    ones = jnp.ones(vt.shape[:2] + (GQA_ONES_ROWS, tk), vt.dtype)
    return jnp.concatenate([vt, ones], axis=2)


NA_GROUP = 4
NA_WIN_ROWS = NA_GROUP + NA_ROWS


def _na_window_start(r0, n_rows):
    return jnp.clip(r0 - NA_ROWS // 2, 0, n_rows - NA_WIN_ROWS)


def _na_bias_rows(n_rows):
    rel = []
    for r0 in (0, NA_WIN_ROWS, n_rows - NA_GROUP):
        ws = int(np.clip(r0 - NA_ROWS // 2, 0, n_rows - NA_WIN_ROWS))
        per_row = []
        for i in range(NA_GROUP):
            rs = int(np.clip(r0 + i - NA_ROWS // 2, 0, n_rows - NA_ROWS))
            per_row.append([ws + wr - (r0 + i) + NA_ROWS - 1 if rs <= ws + wr < rs + NA_ROWS else None
                            for wr in range(NA_WIN_ROWS)])
        rel.append(per_row)
    return rel


def _na_build_bias(rpb_ref, b_ref, n_rows):
    shape = (GRID_W, 2 * GRID_W)
    lane = lax.broadcasted_iota(jnp.int32, shape, 1)
    c = lax.broadcasted_iota(jnp.int32, shape, 0)
    left = lane < GRID_W
    kc = jnp.where(left, lane, lane - GRID_W)
    cs = jnp.clip(c - NA_COLS // 2, 0, GRID_W - NA_COLS)
    col_ok = (kc >= cs) & (kc < cs + NA_COLS)
    neg = jnp.full(shape, NEG_BIG, jnp.float32)
    n_rel = 2 * NA_ROWS - 1
    tiles = []
    for dr in range(n_rel):
        row = jnp.broadcast_to(rpb_ref[dr:dr + 1, :], shape) * LOG2E
        lo = pltpu.roll(row, 2 * GRID_W - (NA_COLS - 1), 1, stride=1, stride_axis=0)
        hi = pltpu.roll(row, GRID_W - (NA_COLS - 1), 1, stride=1, stride_axis=0)
        tiles.append(jnp.where(col_ok, jnp.where(left, lo, hi), neg))
    rel = _na_bias_rows(n_rows)
    for v in range(3):
        for i in range(NA_GROUP):
            for j in range(NA_WIN_ROWS // 2):
                a, b = rel[v][i][2 * j], rel[v][i][2 * j + 1]
                pair = jnp.where(left, neg if a is None else tiles[a], neg if b is None else tiles[b])
                b_ref[v, i * GRID_W:(i + 1) * GRID_W, j * 2 * GRID_W:(j + 1) * 2 * GRID_W] = pair


def _natten_kernel(q_ref, k_ref, v_ref, rpb_ref, o_ref, b_ref, *, groups_per_step, n_rows):
    gq = NA_GROUP * GRID_W
    win = NA_WIN_ROWS * GRID_W

    @pl.when(pl.program_id(1) == 0)
    def _():
        _na_build_bias(rpb_ref, b_ref, n_rows)

    for g in range(groups_per_step):
        r0 = (pl.program_id(1) * groups_per_step + g) * NA_GROUP
        variant = jnp.where(r0 == 0, 0, jnp.where(r0 == n_rows - NA_GROUP, 2, 1))
        k0 = pl.multiple_of(_na_window_start(r0, n_rows) * GRID_W, GRID_W)
        q = q_ref[g * gq:(g + 1) * gq, :]
        kw = k_ref[pl.ds(k0, win), :]
        vw = v_ref[pl.ds(k0, win), :]
        s = lax.dot_general(q, kw, _NT, preferred_element_type=jnp.float32) + b_ref[variant]
        p = jnp.exp2(s - jnp.max(s, axis=-1, keepdims=True))
        l = jnp.sum(p, axis=-1, keepdims=True)
        o = jnp.dot(p.astype(vw.dtype), vw, preferred_element_type=jnp.float32) / l
        o_ref[g * gq:(g + 1) * gq, :] = o.astype(o_ref.dtype)


def _natten(proj, rpb, *, n_heads, q_col, k_col, v_col, groups_per_step=4):
    s = proj.shape[0]
    n_rows = s // GRID_W
    tq = groups_per_step * NA_GROUP * GRID_W
    gq, win = NA_GROUP * GRID_W, NA_WIN_ROWS * GRID_W
    assert n_rows % (groups_per_step * NA_GROUP) == 0 and n_rows >= 2 * NA_WIN_ROWS
    assert NA_WIN_ROWS % 2 == 0 and 2 * GRID_W == HEAD_DIM and rpb.shape[2] <= GRID_W
    rel_rows = -(-rpb.shape[1] // SUBLANES) * SUBLANES
    rpb_rows = jnp.pad(rpb, ((0, 0), (0, rel_rows - rpb.shape[1]), (0, 2 * GRID_W - rpb.shape[2])))
    kern = functools.partial(_natten_kernel, groups_per_step=groups_per_step, n_rows=n_rows)
    return pl.pallas_call(
        kern,
        grid=(n_heads, s // tq),
        in_specs=[pl.BlockSpec((tq, HEAD_DIM), lambda h, i: (i, q_col // HEAD_DIM + h)),
                  pl.BlockSpec((s, HEAD_DIM), lambda h, i: (0, k_col // HEAD_DIM + h)),
                  pl.BlockSpec((s, HEAD_DIM), lambda h, i: (0, v_col // HEAD_DIM + h)),
                  pl.BlockSpec((None, rel_rows, 2 * GRID_W), lambda h, i: (h, 0, 0))],
        out_specs=pl.BlockSpec((tq, HEAD_DIM), lambda h, i: (i, h)),
        out_shape=jax.ShapeDtypeStruct((s, n_heads * HEAD_DIM), jnp.bfloat16),
        scratch_shapes=[pltpu.VMEM((3, gq, win), jnp.float32)],
        compiler_params=_cparams(("arbitrary", "arbitrary")),
        name="natten",
    )(proj, proj, proj, rpb_rows)


def _merge_kernel(ya_ref, yb_ref, ga_ref, gb_ref, woa_ref, wob_ref, wout_ref, x_ref, g1_ref, n2_ref, sc_ref,
                  sh_ref, x1_ref, h2_ref, m_ref, *, tn, n_tiles):
    n = pl.program_id(1)
    a = jnp.dot(ya_ref[...], woa_ref[n], preferred_element_type=jnp.float32)
    b = jnp.dot(yb_ref[...], wob_ref[n], preferred_element_type=jnp.float32)
    m = (jax.nn.sigmoid(ga_ref[...].astype(jnp.float32)) * a
         + jax.nn.sigmoid(gb_ref[...].astype(jnp.float32)) * b)
    col = pl.multiple_of(n * tn, tn)
    m_ref[:, pl.ds(col, tn)] = m.astype(m_ref.dtype)

    @pl.when(n == n_tiles - 1)
    def _():
        y = jnp.dot(m_ref[...], wout_ref[...], preferred_element_type=jnp.float32)
        x1 = x_ref[...] + g1_ref[...] * y
        x1_ref[...] = x1
        h2_ref[...] = _rms_modulate(x1, n2_ref[...], sc_ref[...], sh_ref[...]).astype(h2_ref.dtype)


def _column_tiles(w, tn):
    k, n = w.shape
    return w.reshape(k, n // tn, tn).transpose(1, 0, 2)


def _merge(ya, yb, proj, w_oa, w_ob, w_out, x, g1, n2, sc2, sh2, *, ga_col, gb_col, tm=512, tn=512):
    s, d = x.shape
    da = ya.shape[1]
    n_tiles = d // tn
    kern = functools.partial(_merge_kernel, tn=tn, n_tiles=n_tiles)
    row = lambda i, n: (0, 0)
    tiles = pl.BlockSpec((n_tiles, da, tn), lambda i, n: (0, 0, 0), pipeline_mode=pl.Buffered(1))
    return pl.pallas_call(
        kern,
        grid=(s // tm, n_tiles),
        in_specs=[pl.BlockSpec((tm, da), lambda i, n: (i, 0)),
                  pl.BlockSpec((tm, da), lambda i, n: (i, 0)),
                  pl.BlockSpec((tm, tn), lambda i, n: (i, ga_col // tn + n)),
                  pl.BlockSpec((tm, tn), lambda i, n: (i, gb_col // tn + n)),
                  tiles, tiles,
                  pl.BlockSpec((d, d), row, pipeline_mode=pl.Buffered(1)),
                  pl.BlockSpec((tm, d), lambda i, n: (i, 0)),
                  pl.BlockSpec((1, d), row), pl.BlockSpec((1, d), row),
                  pl.BlockSpec((1, d), row), pl.BlockSpec((1, d), row)],
        out_specs=[pl.BlockSpec((tm, d), lambda i, n: (i, 0)),
                   pl.BlockSpec((tm, d), lambda i, n: (i, 0))],
        out_shape=[jax.ShapeDtypeStruct((s, d), jnp.float32),
                   jax.ShapeDtypeStruct((s, d), jnp.bfloat16)],
        scratch_shapes=[pltpu.VMEM((tm, d), jnp.bfloat16)],
        compiler_params=_cparams(("parallel", "arbitrary")),
        name="merge",
    )(ya, yb, proj, proj, _column_tiles(w_oa, tn), _column_tiles(w_ob, tn), w_out, x, g1, n2, sc2, sh2)


def _ffn_kernel(h_ref, wg_ref, wu_ref, wd_ref, x1_ref, g2_ref, fw_ref, o_ref, acc_ref, *, n_tiles, final_norm):
    f = pl.program_id(1)

    @pl.when(f == 0)
    def _():
        acc_ref[...] = jnp.zeros(acc_ref.shape, jnp.float32)

    h = h_ref[...]
    g = jnp.dot(h, wg_ref[...], preferred_element_type=jnp.float32)
    u = jnp.dot(h, wu_ref[...], preferred_element_type=jnp.float32)
    a = (g * jax.nn.sigmoid(g) * u).astype(h.dtype)
    acc_ref[...] += jnp.dot(a, wd_ref[...], preferred_element_type=jnp.float32)

    @pl.when(f == n_tiles - 1)
    def _():
        x2 = x1_ref[...] + g2_ref[...] * acc_ref[...]
        if final_norm:
            x2 = x2 * lax.rsqrt(jnp.mean(x2 * x2, axis=-1, keepdims=True) + NORM_EPS) * fw_ref[...]
        o_ref[...] = x2


def _ffn(h2, w_gate, w_up, w_down, x1, g2, final_w, *, final_norm, tm=512, tf=512):
    s, d = x1.shape
    dff = w_gate.shape[1]
    n_tiles = dff // tf
    kern = functools.partial(_ffn_kernel, n_tiles=n_tiles, final_norm=final_norm)
    row = lambda i, f: (0, 0)
    return pl.pallas_call(
        kern,
        grid=(s // tm, n_tiles),
        in_specs=[pl.BlockSpec((tm, d), lambda i, f: (i, 0)),
                  pl.BlockSpec((d, tf), lambda i, f: (0, f)),
                  pl.BlockSpec((d, tf), lambda i, f: (0, f)),
                  pl.BlockSpec((tf, d), lambda i, f: (f, 0)),
                  pl.BlockSpec((tm, d), lambda i, f: (i, 0)),
                  pl.BlockSpec((1, d), row), pl.BlockSpec((1, d), row)],
        out_specs=pl.BlockSpec((tm, d), lambda i, f: (i, 0)),
        out_shape=jax.ShapeDtypeStruct((s, d), jnp.float32),
        scratch_shapes=[pltpu.VMEM((tm, d), jnp.float32)],
        compiler_params=_cparams(("parallel", "arbitrary")),
        name="ffn",
    )(h2, w_gate, w_up, w_down, x1, g2, final_w)


def _rope_tables(s):
    n_rows = s // GRID_W
    axis_dim = HEAD_DIM // 2
    inv = ROPE_THETA ** (-jnp.arange(0, axis_dim, 2, dtype=jnp.float32) / axis_dim)
    lane = np.arange(HEAD_DIM)
    inv_lane = inv[(lane // 2) % (axis_dim // 2)]
    by_col = (lane >= axis_dim)[None, None, :]
    even = (lane % 2 == 0)[None, :]
    ang_r = jnp.arange(n_rows, dtype=jnp.float32)[:, None] * inv_lane[None]
    ang_c = jnp.arange(GRID_W, dtype=jnp.float32)[:, None] * inv_lane[None]

    def per_token(fn):
        return jnp.where(by_col, fn(ang_c)[None, :, :], fn(ang_r)[:, None, :]).reshape(s, HEAD_DIM)

    cos, sin = per_token(jnp.cos), per_token(jnp.sin)
    return cos, jnp.where(even, -sin, 0.0), jnp.where(even, 0.0, sin)


def kernel(x, c, w_ada, b_ada, norm1_w, w_in, q_norm_w, k_norm_w, nat_rpb, w_oa, w_ob, w_out, norm2_w,
           w_ffn_gate, w_ffn_up, w_ffn_down, final_w):
    b, s, d = x.shape
    assert b == 1
    depth = w_ada.shape[0]
    n_heads_a = w_oa.shape[1] // HEAD_DIM
    n_heads_b = w_ob.shape[1] // HEAD_DIM
    n_kv_a = n_heads_a // KV_GROUP
    splits = (n_heads_a * HEAD_DIM, n_kv_a * HEAD_DIM, n_kv_a * HEAD_DIM,
              n_heads_b * HEAD_DIM, n_heads_b * HEAD_DIM, n_heads_b * HEAD_DIM, d, d)
    offs = [0] + np.cumsum(splits)[:-1].tolist()
    qa_col, ka_col, va_col, qb_col, kb_col, vb_col, ga_col, gb_col = offs
    assert qa_col == 0

    bf = jnp.bfloat16
    cos, sin_e, sin_o = _rope_tables(s)
    xs = x[0]
    c_col = c.reshape(d, 1)
    for l in range(depth):
        mod = _adaln(c_col, w_ada[l], b_ada[l][None, :])
        sh1, sc1, g1, sh2, sc2, g2 = [mod[:, i * d:(i + 1) * d] for i in range(6)]
        proj = _inproj(xs, norm1_w[l][None, :], sc1, sh1, w_in[l].astype(bf),
                       q_norm_w[l][None, :], k_norm_w[l][None, :], cos, sin_e, sin_o,
                       n_q_heads=n_heads_a, n_k_heads=n_kv_a, qb_cols=(qb_col, kb_col),
                       q_scale=HEAD_DIM ** -0.5 * LOG2E)
        ya = _gqa(proj, _chunk_transposed_values(proj, va_col, n_kv_a, GQA_KEY_CHUNK),
                  n_q_heads=n_heads_a, n_kv_heads=n_kv_a, k_col=ka_col)
        yb = _natten(proj, nat_rpb[l], n_heads=n_heads_b,
                     q_col=qb_col, k_col=kb_col, v_col=vb_col)
        xs, h2 = _merge(ya, yb, proj, w_oa[l].astype(bf), w_ob[l].astype(bf), w_out[l].astype(bf),
                        xs, g1, norm2_w[l][None, :], sc2, sh2, ga_col=ga_col, gb_col=gb_col)
        xs = _ffn(h2, w_ffn_gate[l].astype(bf), w_ffn_up[l].astype(bf), w_ffn_down[l].astype(bf),
                  xs, g2, final_w[None, :], final_norm=(l == depth - 1))
    return xs[None]
```

```python
import functools
import math

import jax
import jax.numpy as jnp
import numpy as np
from jax import lax
from jax.experimental import pallas as pl
from jax.experimental.pallas import tpu as pltpu

HEAD_DIM = 128
KV_GROUP = 4
GRID_W = 64
NA_ROWS = 8
NA_COLS = 16
ROPE_THETA = 10000.0
NORM_EPS = 1e-6
NEG_BIG = -1e30
LOG2E = 1.4426950408889634
SUBLANES = 8
GQA_KEY_CHUNK = 512
GQA_ONES_ROWS = 16
GQA_STEPS_PER_ITER = 2

VMEM_LIMIT = 56 * 1024 * 1024

_NT = (((1,), (1,)), ((), ()))


def _cparams(sem):
    return pltpu.CompilerParams(dimension_semantics=sem, vmem_limit_bytes=VMEM_LIMIT)


def _adaln_kernel(c_ref, w_ref, b_ref, o_ref):
    c = c_ref[...]
    act = c * jax.nn.sigmoid(c)
    o_ref[...] = jnp.sum(act * w_ref[...], axis=0, keepdims=True) + b_ref[...]


def _adaln(c_col, w_ada, b_ada, tn=1024):
    d, n = w_ada.shape
    return pl.pallas_call(
        _adaln_kernel,
        grid=(n // tn,),
        in_specs=[pl.BlockSpec((d, 1), lambda j: (0, 0)),
                  pl.BlockSpec((d, tn), lambda j: (0, j)),
                  pl.BlockSpec((1, tn), lambda j: (0, j))],
        out_specs=pl.BlockSpec((1, tn), lambda j: (0, j)),
        out_shape=jax.ShapeDtypeStruct((1, n), jnp.float32),
        compiler_params=_cparams(("arbitrary",)),
        name="adaln",
    )(c_col, w_ada, b_ada)


def _rms_modulate(x, w, sc, sh):
    rstd = lax.rsqrt(jnp.mean(x * x, axis=-1, keepdims=True) + NORM_EPS)
    return (x * rstd) * (w * (1.0 + sc)) + sh


def _head_norm_rope(t, w, cos, sin_e, sin_o, scale):
    t = t * lax.rsqrt(jnp.mean(t * t, axis=-1, keepdims=True) + NORM_EPS) * w
    nxt = pltpu.roll(t, HEAD_DIM - 1, 1)
    prv = pltpu.roll(t, 1, 1)
    return (t * cos + nxt * sin_e + prv * sin_o) * scale


def _inproj_kernel(x_ref, n1_ref, sc_ref, sh_ref, w_ref, qw_ref, kw_ref, rt_ref, ct_ref,
                   o_ref, vt_ref, h_ref, rope_ref, *, n_q_tiles, heads_per_tile, n_k_heads, qb_tiles, q_scale):
    j = pl.program_id(1)
    tm = x_ref.shape[0]

    @pl.when(j == 0)
    def _():
        h_ref[...] = _rms_modulate(x_ref[...], n1_ref[...], sc_ref[...], sh_ref[...]).astype(h_ref.dtype)
        for k in range(3):
            for r in range(tm // GRID_W):
                rope_ref[k, r * GRID_W:(r + 1) * GRID_W, :] = rt_ref[k, r:r + 1, :] + ct_ref[k]

    acc = jnp.dot(h_ref[...], w_ref[...], preferred_element_type=jnp.float32)

    def rope_heads(n_heads, w, scale):
        cos, se, so = rope_ref[0], rope_ref[1], rope_ref[2]
        for hh in range(heads_per_tile):
            sl = slice(hh * HEAD_DIM, (hh + 1) * HEAD_DIM)
            t = acc[:, sl]
            if hh < n_heads:
                t = _head_norm_rope(t, w, cos, se, so, scale)
            o_ref[:, sl] = t.astype(o_ref.dtype)

    @pl.when(j < n_q_tiles)
    def _():
        rope_heads(heads_per_tile, qw_ref[...], q_scale)

    @pl.when(j == n_q_tiles)
    def _():
        rope_heads(n_k_heads, kw_ref[...], 1.0)
        n_v, n_chunks, _, tk = vt_ref.shape
        for vh in range(n_v):
            col = (n_k_heads + vh) * HEAD_DIM
            for cc in range(n_chunks):
                v = acc[cc * tk:(cc + 1) * tk, col:col + HEAD_DIM]
                vt_ref[vh, cc, :HEAD_DIM, :] = v.T.astype(vt_ref.dtype)
                vt_ref[vh, cc, HEAD_DIM:, :] = jnp.ones((GQA_ONES_ROWS, tk), vt_ref.dtype)

    in_qb = (j >= qb_tiles[0]) & (j < qb_tiles[1])

    @pl.when(in_qb)
    def _():
        o_ref[...] = (acc * q_scale).astype(o_ref.dtype)

    @pl.when((j > n_q_tiles) & jnp.logical_not(in_qb))
    def _():
        o_ref[...] = acc.astype(o_ref.dtype)


def _inproj(x, n1, sc1, sh1, w_in, qw, kw, rope_rows, rope_cols, *, n_q_heads, n_k_heads, qb_cols, q_scale,
            tm=1024, tn=512, tk=GQA_KEY_CHUNK):
    s, d = x.shape
    n = w_in.shape[1]
    heads_per_tile = tn // HEAD_DIM
    assert n_q_heads % heads_per_tile == 0 and 2 * n_k_heads == heads_per_tile
    assert qb_cols[0] % tn == 0 and qb_cols[1] % tn == 0 and qb_cols[0] // tn > n_q_heads // heads_per_tile
    assert tm % tk == 0 and tm % GRID_W == 0 and (tm // GRID_W) % SUBLANES == 0
    kern = functools.partial(_inproj_kernel, n_q_tiles=n_q_heads // heads_per_tile,
                             heads_per_tile=heads_per_tile, n_k_heads=n_k_heads,
                             qb_tiles=(qb_cols[0] // tn, qb_cols[1] // tn), q_scale=q_scale)
    row = lambda i, j: (0, 0)
    vt_rows = HEAD_DIM + GQA_ONES_ROWS
    return pl.pallas_call(
        kern,
        grid=(s // tm, n // tn),
        in_specs=[pl.BlockSpec((tm, d), lambda i, j: (i, 0)),
                  pl.BlockSpec((1, d), row), pl.BlockSpec((1, d), row), pl.BlockSpec((1, d), row),
                  pl.BlockSpec((d, tn), lambda i, j: (0, j)),
                  pl.BlockSpec((1, HEAD_DIM), row), pl.BlockSpec((1, HEAD_DIM), row),
                  pl.BlockSpec((3, tm // GRID_W, HEAD_DIM), lambda i, j: (0, i, 0)),
                  pl.BlockSpec((3, GRID_W, HEAD_DIM), lambda i, j: (0, 0, 0))],
        out_specs=[pl.BlockSpec((tm, tn), lambda i, j: (i, j)),
                   pl.BlockSpec((n_k_heads, tm // tk, vt_rows, tk), lambda i, j: (0, i, 0, 0))],
        out_shape=[jax.ShapeDtypeStruct((s, n), jnp.bfloat16),
                   jax.ShapeDtypeStruct((n_k_heads, s // tk, vt_rows, tk), jnp.bfloat16)],
        scratch_shapes=[pltpu.VMEM((tm, d), jnp.bfloat16),
                        pltpu.VMEM((3, tm, HEAD_DIM), jnp.float32)],
        compiler_params=_cparams(("parallel", "arbitrary")),
        name="inproj",
    )(x, n1, sc1, sh1, w_in, qw, kw, rope_rows, rope_cols)


def _sublane_allreduce(x, op):
    shift = SUBLANES // 2
    while shift:
        x = op(x, pltpu.roll(x, shift, 0))
        shift //= 2
    return x


def _gqa_kernel(q_ref, qn_ref, k_ref, vt_ref, o_ref, qt_ref, s0_ref, s1_ref, p0_ref, p1_ref, x0_ref, x1_ref,
                a0_ref, a1_ref, m_ref, acc_ref, *, tq, tk, n_chunks):
    s_refs, p_refs, x_refs, a_refs = (s0_ref, s1_ref), (p0_ref, p1_ref), (x0_ref, x1_ref), (a0_ref, a1_ref)
    nq = KV_GROUP * tq
    acc_rows = HEAD_DIM + GQA_ONES_ROWS

    def scores(c, slot):
        k = k_ref[pl.ds(pl.multiple_of(c * tk, tk), tk), :]
        s = jnp.dot(k, qt_ref[...], preferred_element_type=jnp.float32)
        s_refs[slot][...] = s
        x_refs[slot][...] = jnp.max(s.reshape(tk // SUBLANES, SUBLANES, nq), axis=0)

    def softmax(slot):
        m_old = m_ref[...]
        m_new = jnp.maximum(m_old, _sublane_allreduce(x_refs[slot][...], jnp.maximum))
        a_refs[slot][...] = jnp.exp2(m_old - m_new)
        m_ref[...] = m_new
        s = s_refs[slot][...].reshape(tk // SUBLANES, SUBLANES, nq)
        p_refs[slot][...] = jnp.exp2(s - m_new[None]).reshape(tk, nq).astype(p_refs[slot].dtype)

    def weighted_values(c, slot):
        acc = acc_ref[...].reshape(acc_rows // SUBLANES, SUBLANES, nq) * a_refs[slot][...][None]
        acc_ref[...] = acc.reshape(acc_rows, nq) + jnp.dot(
            vt_ref[c], p_refs[slot][...], preferred_element_type=jnp.float32)

    def load_queries(src_ref):
        for g in range(KV_GROUP):
            qg = src_ref[:, g * HEAD_DIM:(g + 1) * HEAD_DIM].astype(jnp.float32)
            qt_ref[:, g * tq:(g + 1) * tq] = qg.T.astype(qt_ref.dtype)

    def reset_and_first_softmax():
        m_ref[...] = jnp.full(m_ref.shape, -jnp.inf, jnp.float32)
        acc_ref[...] = jnp.zeros(acc_ref.shape, jnp.float32)
        softmax(0)

    unroll = GQA_STEPS_PER_ITER
    assert unroll % 2 == 0 and (n_chunks - 2) % unroll == 0

    @pl.when(pl.program_id(1) == 0)
    def _():
        load_queries(q_ref)
        scores(0, 0)
        scores(1, 1)
        reset_and_first_softmax()

    def body(i, carry):
        for u in range(unroll):
            t = unroll * i + 1 + u
            cur = (1 + u) % 2
            scores(t + 1, 1 - cur)
            softmax(cur)
            weighted_values(t - 1, 1 - cur)
        return carry

    lax.fori_loop(0, (n_chunks - 2) // unroll, body, 0)
    load_queries(qn_ref)
    scores(0, 0)
    softmax(1)
    weighted_values(n_chunks - 2, 0)
    scores(1, 1)
    weighted_values(n_chunks - 1, 1)
    acc = acc_ref[...]
    inv_l = 1.0 / acc[HEAD_DIM:HEAD_DIM + SUBLANES, :]
    out_t = (acc[:HEAD_DIM, :].reshape(HEAD_DIM // SUBLANES, SUBLANES, nq) * inv_l[None]).reshape(HEAD_DIM, nq)
    for g in range(KV_GROUP):
        o_ref[:, g * HEAD_DIM:(g + 1) * HEAD_DIM] = out_t[:, g * tq:(g + 1) * tq].T.astype(o_ref.dtype)
    reset_and_first_softmax()


def _gqa(proj, vt, *, n_q_heads, n_kv_heads, k_col, tq=1024):
    s = proj.shape[0]
    _, n_chunks, acc_rows, tk = vt.shape
    gw = KV_GROUP * HEAD_DIM
    nq = KV_GROUP * tq
    n_blocks = s // tq
    kern = functools.partial(_gqa_kernel, tq=tq, tk=tk, n_chunks=n_chunks)
    stat = pltpu.VMEM((SUBLANES, nq), jnp.float32)
    return pl.pallas_call(
        kern,
        grid=(n_kv_heads, n_blocks),
        in_specs=[pl.BlockSpec((tq, gw), lambda h, i: (i, h)),
                  pl.BlockSpec((tq, gw), lambda h, i: (jnp.minimum(i + 1, n_blocks - 1), h)),
                  pl.BlockSpec((s, HEAD_DIM), lambda h, i: (0, k_col // HEAD_DIM + h),
                               pipeline_mode=pl.Buffered(1)),
                  pl.BlockSpec((None, n_chunks, acc_rows, tk), lambda h, i: (h, 0, 0, 0),
                               pipeline_mode=pl.Buffered(1))],
        out_specs=pl.BlockSpec((tq, gw), lambda h, i: (i, h)),
        out_shape=jax.ShapeDtypeStruct((s, n_q_heads * HEAD_DIM), jnp.bfloat16),
        scratch_shapes=[pltpu.VMEM((HEAD_DIM, nq), jnp.bfloat16),
                        pltpu.VMEM((tk, nq), jnp.float32), pltpu.VMEM((tk, nq), jnp.float32),
                        pltpu.VMEM((tk, nq), jnp.bfloat16), pltpu.VMEM((tk, nq), jnp.bfloat16),
                        stat, stat, stat, stat, stat,
                        pltpu.VMEM((acc_rows, nq), jnp.float32)],
        compiler_params=_cparams(("arbitrary", "arbitrary")),
        name="gqa",
    )(proj, proj, proj, vt)


NA_GROUP = 4
NA_WIN_ROWS = NA_GROUP + NA_ROWS


def _na_window_start(r0, n_rows):
    return jnp.clip(r0 - NA_ROWS // 2, 0, n_rows - NA_WIN_ROWS)


def _na_bias_rows(n_rows):
    rel = []
    for r0 in (0, NA_WIN_ROWS, n_rows - NA_GROUP):
        ws = int(np.clip(r0 - NA_ROWS // 2, 0, n_rows - NA_WIN_ROWS))
        per_row = []
        for i in range(NA_GROUP):
            rs = int(np.clip(r0 + i - NA_ROWS // 2, 0, n_rows - NA_ROWS))
            per_row.append([ws + wr - (r0 + i) + NA_ROWS - 1 if rs <= ws + wr < rs + NA_ROWS else None
                            for wr in range(NA_WIN_ROWS)])
        rel.append(per_row)
    return rel


def _na_build_bias(rpb_ref, b_ref, n_rows):
    shape = (GRID_W, 2 * GRID_W)
    lane = lax.broadcasted_iota(jnp.int32, shape, 1)
    c = lax.broadcasted_iota(jnp.int32, shape, 0)
    left = lane < GRID_W
    kc = jnp.where(left, lane, lane - GRID_W)
    cs = jnp.clip(c - NA_COLS // 2, 0, GRID_W - NA_COLS)
    col_ok = (kc >= cs) & (kc < cs + NA_COLS)
    neg = jnp.full(shape, NEG_BIG, jnp.float32)
    n_rel = 2 * NA_ROWS - 1
    tiles = []
    for dr in range(n_rel):
        row = jnp.broadcast_to(rpb_ref[dr:dr + 1, :], shape) * LOG2E
        lo = pltpu.roll(row, 2 * GRID_W - (NA_COLS - 1), 1, stride=1, stride_axis=0)
        hi = pltpu.roll(row, GRID_W - (NA_COLS - 1), 1, stride=1, stride_axis=0)
        tiles.append(jnp.where(col_ok, jnp.where(left, lo, hi), neg))
    rel = _na_bias_rows(n_rows)
    for v in range(3):
        for i in range(NA_GROUP):
            for j in range(NA_WIN_ROWS // 2):
                a, b = rel[v][i][2 * j], rel[v][i][2 * j + 1]
                pair = jnp.where(left, neg if a is None else tiles[a], neg if b is None else tiles[b])
                b_ref[v, i * GRID_W:(i + 1) * GRID_W, j * 2 * GRID_W:(j + 1) * 2 * GRID_W] = pair


def _natten_kernel(q_ref, k_ref, v_ref, rpb_ref, o_ref, b_ref, *, groups_per_step, n_rows):
    gq = NA_GROUP * GRID_W
    win = NA_WIN_ROWS * GRID_W

    @pl.when(pl.program_id(1) == 0)
    def _():
        _na_build_bias(rpb_ref, b_ref, n_rows)

    for g in range(groups_per_step):
        r0 = (pl.program_id(1) * groups_per_step + g) * NA_GROUP
        variant = jnp.where(r0 == 0, 0, jnp.where(r0 == n_rows - NA_GROUP, 2, 1))
        k0 = pl.multiple_of(_na_window_start(r0, n_rows) * GRID_W, GRID_W)
        q = q_ref[g * gq:(g + 1) * gq, :]
        kw = k_ref[pl.ds(k0, win), :]
        vw = v_ref[pl.ds(k0, win), :]
        s = lax.dot_general(q, kw, _NT, preferred_element_type=jnp.float32) + b_ref[variant]
        p = jnp.exp2(s - jnp.max(s, axis=-1, keepdims=True))
        l = jnp.sum(p, axis=-1, keepdims=True)
        o = jnp.dot(p.astype(vw.dtype), vw, preferred_element_type=jnp.float32) / l
        o_ref[g * gq:(g + 1) * gq, :] = o.astype(o_ref.dtype)


def _natten(proj, rpb, *, n_heads, q_col, k_col, v_col, groups_per_step=4):
    s = proj.shape[0]
    n_rows = s // GRID_W
    tq = groups_per_step * NA_GROUP * GRID_W
    gq, win = NA_GROUP * GRID_W, NA_WIN_ROWS * GRID_W
    assert n_rows % (groups_per_step * NA_GROUP) == 0 and n_rows >= 2 * NA_WIN_ROWS
    assert NA_WIN_ROWS % 2 == 0 and 2 * GRID_W == HEAD_DIM and rpb.shape[2] <= GRID_W
    rel_rows = -(-rpb.shape[1] // SUBLANES) * SUBLANES
    rpb_rows = jnp.pad(rpb, ((0, 0), (0, rel_rows - rpb.shape[1]), (0, 2 * GRID_W - rpb.shape[2])))
    kern = functools.partial(_natten_kernel, groups_per_step=groups_per_step, n_rows=n_rows)
    return pl.pallas_call(
        kern,
        grid=(n_heads, s // tq),
        in_specs=[pl.BlockSpec((tq, HEAD_DIM), lambda h, i: (i, q_col // HEAD_DIM + h)),
                  pl.BlockSpec((s, HEAD_DIM), lambda h, i: (0, k_col // HEAD_DIM + h)),
                  pl.BlockSpec((s, HEAD_DIM), lambda h, i: (0, v_col // HEAD_DIM + h)),
                  pl.BlockSpec((None, rel_rows, 2 * GRID_W), lambda h, i: (h, 0, 0))],
        out_specs=pl.BlockSpec((tq, HEAD_DIM), lambda h, i: (i, h)),
        out_shape=jax.ShapeDtypeStruct((s, n_heads * HEAD_DIM), jnp.bfloat16),
        scratch_shapes=[pltpu.VMEM((3, gq, win), jnp.float32)],
        compiler_params=_cparams(("arbitrary", "arbitrary")),
        name="natten",
    )(proj, proj, proj, rpb_rows)


def _merge_kernel(ya_ref, yb_ref, ga_ref, gb_ref, woa_ref, wob_ref, wout_ref, x_ref, g1_ref, n2_ref, sc_ref,
                  sh_ref, x1_ref, h2_ref, m_ref, *, tn, n_tiles):
    n = pl.program_id(1)
    a = jnp.dot(ya_ref[...], woa_ref[n], preferred_element_type=jnp.float32)
    b = jnp.dot(yb_ref[...], wob_ref[n], preferred_element_type=jnp.float32)
    m = (jax.nn.sigmoid(ga_ref[...].astype(jnp.float32)) * a
         + jax.nn.sigmoid(gb_ref[...].astype(jnp.float32)) * b)
    col = pl.multiple_of(n * tn, tn)
    m_ref[:, pl.ds(col, tn)] = m.astype(m_ref.dtype)

    @pl.when(n == n_tiles - 1)
    def _():
        y = jnp.dot(m_ref[...], wout_ref[...], preferred_element_type=jnp.float32)
        x1 = x_ref[...] + g1_ref[...] * y
        x1_ref[...] = x1
        h2_ref[...] = _rms_modulate(x1, n2_ref[...], sc_ref[...], sh_ref[...]).astype(h2_ref.dtype)


def _column_tiles(w, tn):
    k, n = w.shape
    return w.reshape(k, n // tn, tn).transpose(1, 0, 2)


def _merge(ya, yb, proj, w_oa, w_ob, w_out, x, g1, n2, sc2, sh2, *, ga_col, gb_col, tm=512, tn=512):
    s, d = x.shape
    da = ya.shape[1]
    n_tiles = d // tn
    kern = functools.partial(_merge_kernel, tn=tn, n_tiles=n_tiles)
    row = lambda i, n: (0, 0)
    tiles = pl.BlockSpec((n_tiles, da, tn), lambda i, n: (0, 0, 0), pipeline_mode=pl.Buffered(1))
    return pl.pallas_call(
        kern,
        grid=(s // tm, n_tiles),
        in_specs=[pl.BlockSpec((tm, da), lambda i, n: (i, 0)),
                  pl.BlockSpec((tm, da), lambda i, n: (i, 0)),
                  pl.BlockSpec((tm, tn), lambda i, n: (i, ga_col // tn + n)),
                  pl.BlockSpec((tm, tn), lambda i, n: (i, gb_col // tn + n)),
                  tiles, tiles,
                  pl.BlockSpec((d, d), row, pipeline_mode=pl.Buffered(1)),
                  pl.BlockSpec((tm, d), lambda i, n: (i, 0)),
                  pl.BlockSpec((1, d), row), pl.BlockSpec((1, d), row),
                  pl.BlockSpec((1, d), row), pl.BlockSpec((1, d), row)],
        out_specs=[pl.BlockSpec((tm, d), lambda i, n: (i, 0)),
                   pl.BlockSpec((tm, d), lambda i, n: (i, 0))],
        out_shape=[jax.ShapeDtypeStruct((s, d), jnp.float32),
                   jax.ShapeDtypeStruct((s, d), jnp.bfloat16)],
        scratch_shapes=[pltpu.VMEM((tm, d), jnp.bfloat16)],
        compiler_params=_cparams(("parallel", "arbitrary")),
        name="merge",
    )(ya, yb, proj, proj, _column_tiles(w_oa, tn), _column_tiles(w_ob, tn), w_out, x, g1, n2, sc2, sh2)


def _ffn_kernel(h_ref, wg_ref, wu_ref, wd_ref, x1_ref, g2_ref, fw_ref, o_ref, acc_ref, a0_ref, a1_ref,
                *, n_tiles, final_norm):
    f = pl.program_id(1)
    a_refs = (a0_ref, a1_ref)

    def gate_up(dst_ref):
        h = h_ref[...]
        g = jnp.dot(h, wg_ref[...], preferred_element_type=jnp.float32)
        u = jnp.dot(h, wu_ref[...], preferred_element_type=jnp.float32)
        dst_ref[...] = (g * jax.nn.sigmoid(g) * u).astype(dst_ref.dtype)

    def down(src_ref):
        acc_ref[...] += jnp.dot(src_ref[...], wd_ref[...], preferred_element_type=jnp.float32)

    @pl.when(f == 0)
    def _():
        acc_ref[...] = jnp.zeros(acc_ref.shape, jnp.float32)
        gate_up(a_refs[0])

    for parity in range(2):
        @pl.when((f > 0) & (f < n_tiles) & (f % 2 == parity))
        def _():
            down(a_refs[1 - parity])
            gate_up(a_refs[parity])

    @pl.when(f == n_tiles)
    def _():
        down(a_refs[(n_tiles - 1) % 2])
        x2 = x1_ref[...] + g2_ref[...] * acc_ref[...]
        if final_norm:
            x2 = x2 * lax.rsqrt(jnp.mean(x2 * x2, axis=-1, keepdims=True) + NORM_EPS) * fw_ref[...]
        o_ref[...] = x2


def _ffn(h2, w_gate, w_up, w_down, x1, g2, final_w, *, final_norm, tm=512, tf=512):
    s, d = x1.shape
    dff = w_gate.shape[1]
    n_tiles = dff // tf
    kern = functools.partial(_ffn_kernel, n_tiles=n_tiles, final_norm=final_norm)
    row = lambda i, f: (0, 0)
    this_tile = lambda i, f: (0, jnp.minimum(f, n_tiles - 1))
    return pl.pallas_call(
        kern,
        grid=(s // tm, n_tiles + 1),
        in_specs=[pl.BlockSpec((tm, d), lambda i, f: (i, 0)),
                  pl.BlockSpec((d, tf), this_tile),
                  pl.BlockSpec((d, tf), this_tile),
                  pl.BlockSpec((tf, d), lambda i, f: (jnp.maximum(f - 1, 0), 0)),
                  pl.BlockSpec((tm, d), lambda i, f: (i, 0)),
                  pl.BlockSpec((1, d), row), pl.BlockSpec((1, d), row)],
        out_specs=pl.BlockSpec((tm, d), lambda i, f: (i, 0)),
        out_shape=jax.ShapeDtypeStruct((s, d), jnp.float32),
        scratch_shapes=[pltpu.VMEM((tm, d), jnp.float32),
                        pltpu.VMEM((tm, tf), jnp.bfloat16), pltpu.VMEM((tm, tf), jnp.bfloat16)],
        compiler_params=_cparams(("parallel", "arbitrary")),
        name="ffn",
    )(h2, w_gate, w_up, w_down, x1, g2, final_w)


def _rope_tables(s):
    n_rows = s // GRID_W
    axis_dim = HEAD_DIM // 2
    inv = ROPE_THETA ** (-jnp.arange(0, axis_dim, 2, dtype=jnp.float32) / axis_dim)
    lane = np.arange(HEAD_DIM)
    inv_lane = inv[(lane // 2) % (axis_dim // 2)]
    by_col = (lane >= axis_dim)[None, :]
    even = (lane % 2 == 0)[None, :]

    def parts(n, mine):
        ang = jnp.arange(n, dtype=jnp.float32)[:, None] * inv_lane[None]
        cos, sin = jnp.cos(ang), jnp.sin(ang)
        tabs = jnp.stack([cos, jnp.where(even, -sin, 0.0), jnp.where(even, 0.0, sin)])
        return jnp.where(mine[None], tabs, 0.0)

    return parts(n_rows, ~by_col), parts(GRID_W, by_col)


def kernel(x, c, w_ada, b_ada, norm1_w, w_in, q_norm_w, k_norm_w, nat_rpb, w_oa, w_ob, w_out, norm2_w,
           w_ffn_gate, w_ffn_up, w_ffn_down, final_w):
    b, s, d = x.shape
    assert b == 1
    depth = w_ada.shape[0]
    n_heads_a = w_oa.shape[1] // HEAD_DIM
    n_heads_b = w_ob.shape[1] // HEAD_DIM
    n_kv_a = n_heads_a // KV_GROUP
    splits = (n_heads_a * HEAD_DIM, n_kv_a * HEAD_DIM, n_kv_a * HEAD_DIM,
              n_heads_b * HEAD_DIM, n_heads_b * HEAD_DIM, n_heads_b * HEAD_DIM, d, d)
    offs = [0] + np.cumsum(splits)[:-1].tolist()
    qa_col, ka_col, va_col, qb_col, kb_col, vb_col, ga_col, gb_col = offs
    assert qa_col == 0

    bf = jnp.bfloat16
    assert va_col == ka_col + n_kv_a * HEAD_DIM
    rope_rows, rope_cols = _rope_tables(s)
    xs = x[0]
    c_col = c.reshape(d, 1)
    for l in range(depth):
        mod = _adaln(c_col, w_ada[l], b_ada[l][None, :])
        sh1, sc1, g1, sh2, sc2, g2 = [mod[:, i * d:(i + 1) * d] for i in range(6)]
        proj, vt = _inproj(xs, norm1_w[l][None, :], sc1, sh1, w_in[l].astype(bf),
                           q_norm_w[l][None, :], k_norm_w[l][None, :], rope_rows, rope_cols,
                           n_q_heads=n_heads_a, n_k_heads=n_kv_a, qb_cols=(qb_col, kb_col),
                           q_scale=HEAD_DIM ** -0.5 * LOG2E)
        ya = _gqa(proj, vt, n_q_heads=n_heads_a, n_kv_heads=n_kv_a, k_col=ka_col)
        yb = _natten(proj, nat_rpb[l], n_heads=n_heads_b,
                     q_col=qb_col, k_col=kb_col, v_col=vb_col)
        xs, h2 = _merge(ya, yb, proj, w_oa[l].astype(bf), w_ob[l].astype(bf), w_out[l].astype(bf),
                        xs, g1, norm2_w[l][None, :], sc2, sh2, ga_col=ga_col, gb_col=gb_col)
        xs = _ffn(h2, w_ffn_gate[l].astype(bf), w_ffn_up[l].astype(bf), w_ffn_down[l].astype(bf),
                  xs, g2, final_w[None, :], final_norm=(l == depth - 1))
    return xs[None]
```

```python
import functools
import math

import jax
import jax.numpy as jnp
import numpy as np
from jax import lax
from jax.experimental import pallas as pl
from jax.experimental.pallas import tpu as pltpu

HEAD_DIM = 128
KV_GROUP = 4
GRID_W = 64
NA_ROWS = 8
NA_COLS = 16
ROPE_THETA = 10000.0
NORM_EPS = 1e-6
NEG_BIG = -1e30
LOG2E = 1.4426950408889634
SUBLANES = 8
GQA_KEY_CHUNK = 512
GQA_ONES_ROWS = 16
GQA_STEPS_PER_ITER = 2

VMEM_LIMIT = 56 * 1024 * 1024

_NT = (((1,), (1,)), ((), ()))


def _cparams(sem):
    return pltpu.CompilerParams(dimension_semantics=sem, vmem_limit_bytes=VMEM_LIMIT)


def _adaln_kernel(c_ref, w_ref, b_ref, o_ref):
    c = c_ref[...]
    act = c * jax.nn.sigmoid(c)
    o_ref[...] = jnp.sum(act * w_ref[...], axis=0, keepdims=True) + b_ref[...]


def _adaln(c_col, w_ada, b_ada, tn=1024):
    d, n = w_ada.shape
    return pl.pallas_call(
        _adaln_kernel,
        grid=(n // tn,),
        in_specs=[pl.BlockSpec((d, 1), lambda j: (0, 0)),
                  pl.BlockSpec((d, tn), lambda j: (0, j)),
                  pl.BlockSpec((1, tn), lambda j: (0, j))],
        out_specs=pl.BlockSpec((1, tn), lambda j: (0, j)),
        out_shape=jax.ShapeDtypeStruct((1, n), jnp.float32),
        compiler_params=_cparams(("arbitrary",)),
        name="adaln",
    )(c_col, w_ada, b_ada)


def _rms_modulate(x, w, sc, sh):
    rstd = lax.rsqrt(jnp.mean(x * x, axis=-1, keepdims=True) + NORM_EPS)
    return (x * rstd) * (w * (1.0 + sc)) + sh


def _head_norm_rope(t, w, cos, sin_e, sin_o, scale):
    t = t * lax.rsqrt(jnp.mean(t * t, axis=-1, keepdims=True) + NORM_EPS) * w
    nxt = pltpu.roll(t, HEAD_DIM - 1, 1)
    prv = pltpu.roll(t, 1, 1)
    return (t * cos + nxt * sin_e + prv * sin_o) * scale


def _inproj_kernel(x_ref, n1_ref, sc_ref, sh_ref, w_ref, qw_ref, kw_ref, rt_ref, ct_ref,
                   o_ref, vt_ref, h_ref, rope_ref, *, n_q_tiles, heads_per_tile, n_k_heads, qb_tiles, q_scale):
    j = pl.program_id(1)
    tm = x_ref.shape[0]

    @pl.when(j == 0)
    def _():
        h_ref[...] = _rms_modulate(x_ref[...], n1_ref[...], sc_ref[...], sh_ref[...]).astype(h_ref.dtype)
        for k in range(3):
            for r in range(tm // GRID_W):
                rope_ref[k, r * GRID_W:(r + 1) * GRID_W, :] = rt_ref[k, r:r + 1, :] + ct_ref[k]

    acc = jnp.dot(h_ref[...], w_ref[...], preferred_element_type=jnp.float32)

    def rope_heads(n_heads, w, scale):
        cos, se, so = rope_ref[0], rope_ref[1], rope_ref[2]
        for hh in range(heads_per_tile):
            sl = slice(hh * HEAD_DIM, (hh + 1) * HEAD_DIM)
            t = acc[:, sl]
            if hh < n_heads:
                t = _head_norm_rope(t, w, cos, se, so, scale)
            o_ref[:, sl] = t.astype(o_ref.dtype)

    @pl.when(j < n_q_tiles)
    def _():
        rope_heads(heads_per_tile, qw_ref[...], q_scale)

    @pl.when(j == n_q_tiles)
    def _():
        rope_heads(n_k_heads, kw_ref[...], 1.0)
        n_v, n_chunks, _, tk = vt_ref.shape
        for vh in range(n_v):
            col = (n_k_heads + vh) * HEAD_DIM
            for cc in range(n_chunks):
                v = acc[cc * tk:(cc + 1) * tk, col:col + HEAD_DIM]
                vt_ref[vh, cc, :HEAD_DIM, :] = v.T.astype(vt_ref.dtype)
                vt_ref[vh, cc, HEAD_DIM:, :] = jnp.ones((GQA_ONES_ROWS, tk), vt_ref.dtype)

    in_qb = (j >= qb_tiles[0]) & (j < qb_tiles[1])

    @pl.when(in_qb)
    def _():
        o_ref[...] = (acc * q_scale).astype(o_ref.dtype)

    @pl.when((j > n_q_tiles) & jnp.logical_not(in_qb))
    def _():
        o_ref[...] = acc.astype(o_ref.dtype)


def _inproj(x, n1, sc1, sh1, w_in, qw, kw, rope_rows, rope_cols, *, n_q_heads, n_k_heads, qb_cols, q_scale,
            tm=1024, tn=512, tk=GQA_KEY_CHUNK):
    s, d = x.shape
    n = w_in.shape[1]
    heads_per_tile = tn // HEAD_DIM
    assert n_q_heads % heads_per_tile == 0 and 2 * n_k_heads == heads_per_tile
    assert qb_cols[0] % tn == 0 and qb_cols[1] % tn == 0 and qb_cols[0] // tn > n_q_heads // heads_per_tile
    assert tm % tk == 0 and tm % GRID_W == 0 and (tm // GRID_W) % SUBLANES == 0
    kern = functools.partial(_inproj_kernel, n_q_tiles=n_q_heads // heads_per_tile,
                             heads_per_tile=heads_per_tile, n_k_heads=n_k_heads,
                             qb_tiles=(qb_cols[0] // tn, qb_cols[1] // tn), q_scale=q_scale)
    row = lambda i, j: (0, 0)
    vt_rows = HEAD_DIM + GQA_ONES_ROWS
    return pl.pallas_call(
        kern,
        grid=(s // tm, n // tn),
        in_specs=[pl.BlockSpec((tm, d), lambda i, j: (i, 0)),
                  pl.BlockSpec((1, d), row), pl.BlockSpec((1, d), row), pl.BlockSpec((1, d), row),
                  pl.BlockSpec((d, tn), lambda i, j: (0, j)),
                  pl.BlockSpec((1, HEAD_DIM), row), pl.BlockSpec((1, HEAD_DIM), row),
                  pl.BlockSpec((3, tm // GRID_W, HEAD_DIM), lambda i, j: (0, i, 0)),
                  pl.BlockSpec((3, GRID_W, HEAD_DIM), lambda i, j: (0, 0, 0))],
        out_specs=[pl.BlockSpec((tm, tn), lambda i, j: (i, j)),
                   pl.BlockSpec((n_k_heads, tm // tk, vt_rows, tk), lambda i, j: (0, i, 0, 0))],
        out_shape=[jax.ShapeDtypeStruct((s, n), jnp.bfloat16),
                   jax.ShapeDtypeStruct((n_k_heads, s // tk, vt_rows, tk), jnp.bfloat16)],
        scratch_shapes=[pltpu.VMEM((tm, d), jnp.bfloat16),
                        pltpu.VMEM((3, tm, HEAD_DIM), jnp.float32)],
        compiler_params=_cparams(("parallel", "arbitrary")),
        name="inproj",
    )(x, n1, sc1, sh1, w_in, qw, kw, rope_rows, rope_cols)


def _sublane_allreduce(x, op):
    shift = SUBLANES // 2
    while shift:
        x = op(x, pltpu.roll(x, shift, 0))
        shift //= 2
    return x


def _gqa_kernel(q_ref, qn_ref, k_ref, vt_ref, o_ref, qt_ref, s0_ref, s1_ref, p0_ref, p1_ref, x0_ref, x1_ref,
                a0_ref, a1_ref, m_ref, acc_ref, *, tq, tk, n_chunks):
    s_refs, p_refs, x_refs, a_refs = (s0_ref, s1_ref), (p0_ref, p1_ref), (x0_ref, x1_ref), (a0_ref, a1_ref)
    nq = KV_GROUP * tq
    acc_rows = HEAD_DIM + GQA_ONES_ROWS

    def scores(c, slot):
        k = k_ref[pl.ds(pl.multiple_of(c * tk, tk), tk), :]
        s = jnp.dot(k, qt_ref[...], preferred_element_type=jnp.float32)
        s_refs[slot][...] = s
        x_refs[slot][...] = jnp.max(s.reshape(tk // SUBLANES, SUBLANES, nq), axis=0)

    def softmax(slot):
        m_old = m_ref[...]
        m_new = jnp.maximum(m_old, _sublane_allreduce(x_refs[slot][...], jnp.maximum))
        a_refs[slot][...] = jnp.exp2(m_old - m_new)
        m_ref[...] = m_new
        s = s_refs[slot][...].reshape(tk // SUBLANES, SUBLANES, nq)
        p_refs[slot][...] = jnp.exp2(s - m_new[None]).reshape(tk, nq).astype(p_refs[slot].dtype)

    def weighted_values(c, slot):
        acc = acc_ref[...].reshape(acc_rows // SUBLANES, SUBLANES, nq) * a_refs[slot][...][None]
        acc_ref[...] = acc.reshape(acc_rows, nq) + jnp.dot(
            vt_ref[c], p_refs[slot][...], preferred_element_type=jnp.float32)

    def load_queries(src_ref):
        for g in range(KV_GROUP):
            qg = src_ref[:, g * HEAD_DIM:(g + 1) * HEAD_DIM].astype(jnp.float32)
            qt_ref[:, g * tq:(g + 1) * tq] = qg.T.astype(qt_ref.dtype)

    def reset_and_first_softmax():
        m_ref[...] = jnp.full(m_ref.shape, -jnp.inf, jnp.float32)
        acc_ref[...] = jnp.zeros(acc_ref.shape, jnp.float32)
        softmax(0)

    unroll = GQA_STEPS_PER_ITER
    assert unroll % 2 == 0 and (n_chunks - 2) % unroll == 0

    @pl.when(pl.program_id(1) == 0)
    def _():
        load_queries(q_ref)
        scores(0, 0)
        scores(1, 1)
        reset_and_first_softmax()

    def body(i, carry):
        for u in range(unroll):
            t = unroll * i + 1 + u
            cur = (1 + u) % 2
            scores(t + 1, 1 - cur)
            softmax(cur)
            weighted_values(t - 1, 1 - cur)
        return carry

    lax.fori_loop(0, (n_chunks - 2) // unroll, body, 0)
    load_queries(qn_ref)
    scores(0, 0)
    softmax(1)
    weighted_values(n_chunks - 2, 0)
    scores(1, 1)
    weighted_values(n_chunks - 1, 1)
    acc = acc_ref[...]
    inv_l = 1.0 / acc[HEAD_DIM:HEAD_DIM + SUBLANES, :]
    out_t = (acc[:HEAD_DIM, :].reshape(HEAD_DIM // SUBLANES, SUBLANES, nq) * inv_l[None]).reshape(HEAD_DIM, nq)
    for g in range(KV_GROUP):
        o_ref[:, g * HEAD_DIM:(g + 1) * HEAD_DIM] = out_t[:, g * tq:(g + 1) * tq].T.astype(o_ref.dtype)
    reset_and_first_softmax()


def _gqa(proj, vt, *, n_q_heads, n_kv_heads, k_col, tq=1024):
    s = proj.shape[0]
    _, n_chunks, acc_rows, tk = vt.shape
    gw = KV_GROUP * HEAD_DIM
    nq = KV_GROUP * tq
    n_blocks = s // tq
    kern = functools.partial(_gqa_kernel, tq=tq, tk=tk, n_chunks=n_chunks)
    stat = pltpu.VMEM((SUBLANES, nq), jnp.float32)
    return pl.pallas_call(
        kern,
        grid=(n_kv_heads, n_blocks),
        in_specs=[pl.BlockSpec((tq, gw), lambda h, i: (i, h)),
                  pl.BlockSpec((tq, gw), lambda h, i: (jnp.minimum(i + 1, n_blocks - 1), h)),
                  pl.BlockSpec((s, HEAD_DIM), lambda h, i: (0, k_col // HEAD_DIM + h),
                               pipeline_mode=pl.Buffered(1)),
                  pl.BlockSpec((None, n_chunks, acc_rows, tk), lambda h, i: (h, 0, 0, 0),
                               pipeline_mode=pl.Buffered(1))],
        out_specs=pl.BlockSpec((tq, gw), lambda h, i: (i, h)),
        out_shape=jax.ShapeDtypeStruct((s, n_q_heads * HEAD_DIM), jnp.bfloat16),
        scratch_shapes=[pltpu.VMEM((HEAD_DIM, nq), jnp.bfloat16),
                        pltpu.VMEM((tk, nq), jnp.float32), pltpu.VMEM((tk, nq), jnp.float32),
                        pltpu.VMEM((tk, nq), jnp.bfloat16), pltpu.VMEM((tk, nq), jnp.bfloat16),
                        stat, stat, stat, stat, stat,
                        pltpu.VMEM((acc_rows, nq), jnp.float32)],
        compiler_params=_cparams(("arbitrary", "arbitrary")),
        name="gqa",
    )(proj, proj, proj, vt)


NA_GROUP = 4
NA_WIN_ROWS = NA_GROUP + NA_ROWS


def _na_window_start(r0, n_rows):
    return jnp.clip(r0 - NA_ROWS // 2, 0, n_rows - NA_WIN_ROWS)


def _na_bias_rows(n_rows):
    rel = []
    for r0 in (0, NA_WIN_ROWS, n_rows - NA_GROUP):
        ws = int(np.clip(r0 - NA_ROWS // 2, 0, n_rows - NA_WIN_ROWS))
        per_row = []
        for i in range(NA_GROUP):
            rs = int(np.clip(r0 + i - NA_ROWS // 2, 0, n_rows - NA_ROWS))
            per_row.append([ws + wr - (r0 + i) + NA_ROWS - 1 if rs <= ws + wr < rs + NA_ROWS else None
                            for wr in range(NA_WIN_ROWS)])
        rel.append(per_row)
    return rel


def _na_build_bias(rpb_ref, b_ref, n_rows):
    shape = (GRID_W, 2 * GRID_W)
    lane = lax.broadcasted_iota(jnp.int32, shape, 1)
    c = lax.broadcasted_iota(jnp.int32, shape, 0)
    left = lane < GRID_W
    kc = jnp.where(left, lane, lane - GRID_W)
    cs = jnp.clip(c - NA_COLS // 2, 0, GRID_W - NA_COLS)
    col_ok = (kc >= cs) & (kc < cs + NA_COLS)
    neg = jnp.full(shape, NEG_BIG, jnp.float32)
    n_rel = 2 * NA_ROWS - 1
    tiles = []
    for dr in range(n_rel):
        row = jnp.broadcast_to(rpb_ref[dr:dr + 1, :], shape) * LOG2E
        lo = pltpu.roll(row, 2 * GRID_W - (NA_COLS - 1), 1, stride=1, stride_axis=0)
        hi = pltpu.roll(row, GRID_W - (NA_COLS - 1), 1, stride=1, stride_axis=0)
        tiles.append(jnp.where(col_ok, jnp.where(left, lo, hi), neg))
    rel = _na_bias_rows(n_rows)
    for v in range(3):
        for i in range(NA_GROUP):
            for j in range(NA_WIN_ROWS // 2):
                a, b = rel[v][i][2 * j], rel[v][i][2 * j + 1]
                pair = jnp.where(left, neg if a is None else tiles[a], neg if b is None else tiles[b])
                b_ref[v, i * GRID_W:(i + 1) * GRID_W, j * 2 * GRID_W:(j + 1) * 2 * GRID_W] = pair


def _natten_kernel(q_ref, k_ref, v_ref, rpb_ref, o_ref, b_ref, *, groups_per_step, n_rows):
    gq = NA_GROUP * GRID_W
    win = NA_WIN_ROWS * GRID_W

    @pl.when(pl.program_id(1) == 0)
    def _():
        _na_build_bias(rpb_ref, b_ref, n_rows)

    for g in range(groups_per_step):
        r0 = (pl.program_id(1) * groups_per_step + g) * NA_GROUP
        variant = jnp.where(r0 == 0, 0, jnp.where(r0 == n_rows - NA_GROUP, 2, 1))
        k0 = pl.multiple_of(_na_window_start(r0, n_rows) * GRID_W, GRID_W)
        q = q_ref[g * gq:(g + 1) * gq, :]
        kw = k_ref[pl.ds(k0, win), :]
        vw = v_ref[pl.ds(k0, win), :]
        s = lax.dot_general(q, kw, _NT, preferred_element_type=jnp.float32) + b_ref[variant]
        p = jnp.exp2(s - jnp.max(s, axis=-1, keepdims=True))
        l = jnp.sum(p, axis=-1, keepdims=True)
        o = jnp.dot(p.astype(vw.dtype), vw, preferred_element_type=jnp.float32) / l
        o_ref[g * gq:(g + 1) * gq, :] = o.astype(o_ref.dtype)


def _natten(proj, rpb, *, n_heads, q_col, k_col, v_col, groups_per_step=4):
    s = proj.shape[0]
    n_rows = s // GRID_W
    tq = groups_per_step * NA_GROUP * GRID_W
    gq, win = NA_GROUP * GRID_W, NA_WIN_ROWS * GRID_W
    assert n_rows % (groups_per_step * NA_GROUP) == 0 and n_rows >= 2 * NA_WIN_ROWS
    assert NA_WIN_ROWS % 2 == 0 and 2 * GRID_W == HEAD_DIM and rpb.shape[2] <= GRID_W
    rel_rows = -(-rpb.shape[1] // SUBLANES) * SUBLANES
    rpb_rows = jnp.pad(rpb, ((0, 0), (0, rel_rows - rpb.shape[1]), (0, 2 * GRID_W - rpb.shape[2])))
    kern = functools.partial(_natten_kernel, groups_per_step=groups_per_step, n_rows=n_rows)
    return pl.pallas_call(
        kern,
        grid=(n_heads, s // tq),
        in_specs=[pl.BlockSpec((tq, HEAD_DIM), lambda h, i: (i, q_col // HEAD_DIM + h)),
                  pl.BlockSpec((s, HEAD_DIM), lambda h, i: (0, k_col // HEAD_DIM + h)),
                  pl.BlockSpec((s, HEAD_DIM), lambda h, i: (0, v_col // HEAD_DIM + h)),
                  pl.BlockSpec((None, rel_rows, 2 * GRID_W), lambda h, i: (h, 0, 0))],
        out_specs=pl.BlockSpec((tq, HEAD_DIM), lambda h, i: (i, h)),
        out_shape=jax.ShapeDtypeStruct((s, n_heads * HEAD_DIM), jnp.bfloat16),
        scratch_shapes=[pltpu.VMEM((3, gq, win), jnp.float32)],
        compiler_params=_cparams(("arbitrary", "arbitrary")),
        name="natten",
    )(proj, proj, proj, rpb_rows)


def _merge_kernel(ya_ref, yb_ref, ga_ref, gb_ref, woa_ref, wob_ref, wout_ref, x_ref, g1_ref, n2_ref, sc_ref,
                  sh_ref, x1_ref, h2_ref, m_ref, *, tn, n_tiles):
    n = pl.program_id(1)
    a = jnp.dot(ya_ref[...], woa_ref[n], preferred_element_type=jnp.float32)
    b = jnp.dot(yb_ref[...], wob_ref[n], preferred_element_type=jnp.float32)
    m = (jax.nn.sigmoid(ga_ref[...].astype(jnp.float32)) * a
         + jax.nn.sigmoid(gb_ref[...].astype(jnp.float32)) * b)
    col = pl.multiple_of(n * tn, tn)
    m_ref[:, pl.ds(col, tn)] = m.astype(m_ref.dtype)

    @pl.when(n == n_tiles - 1)
    def _():
        y = jnp.dot(m_ref[...], wout_ref[...], preferred_element_type=jnp.float32)
        x1 = x_ref[...] + g1_ref[...] * y
        x1_ref[...] = x1
        h2_ref[...] = _rms_modulate(x1, n2_ref[...], sc_ref[...], sh_ref[...]).astype(h2_ref.dtype)


def _column_tiles(w, tn):
    k, n = w.shape
    return w.reshape(k, n // tn, tn).transpose(1, 0, 2)


def _merge(ya, yb, proj, w_oa, w_ob, w_out, x, g1, n2, sc2, sh2, *, ga_col, gb_col, tm=512, tn=512):
    s, d = x.shape
    da = ya.shape[1]
    n_tiles = d // tn
    kern = functools.partial(_merge_kernel, tn=tn, n_tiles=n_tiles)
    row = lambda i, n: (0, 0)
    tiles = pl.BlockSpec((n_tiles, da, tn), lambda i, n: (0, 0, 0), pipeline_mode=pl.Buffered(1))
    return pl.pallas_call(
        kern,
        grid=(s // tm, n_tiles),
        in_specs=[pl.BlockSpec((tm, da), lambda i, n: (i, 0)),
                  pl.BlockSpec((tm, da), lambda i, n: (i, 0)),
                  pl.BlockSpec((tm, tn), lambda i, n: (i, ga_col // tn + n)),
                  pl.BlockSpec((tm, tn), lambda i, n: (i, gb_col // tn + n)),
                  tiles, tiles,
                  pl.BlockSpec((d, d), row, pipeline_mode=pl.Buffered(1)),
                  pl.BlockSpec((tm, d), lambda i, n: (i, 0)),
                  pl.BlockSpec((1, d), row), pl.BlockSpec((1, d), row),
                  pl.BlockSpec((1, d), row), pl.BlockSpec((1, d), row)],
        out_specs=[pl.BlockSpec((tm, d), lambda i, n: (i, 0)),
                   pl.BlockSpec((tm, d), lambda i, n: (i, 0))],
        out_shape=[jax.ShapeDtypeStruct((s, d), jnp.float32),
                   jax.ShapeDtypeStruct((s, d), jnp.bfloat16)],
        scratch_shapes=[pltpu.VMEM((tm, d), jnp.bfloat16)],
        compiler_params=_cparams(("parallel", "arbitrary")),
        name="merge",
    )(ya, yb, proj, proj, _column_tiles(w_oa, tn), _column_tiles(w_ob, tn), w_out, x, g1, n2, sc2, sh2)


def _ffn_kernel(h_ref, wg_ref, wu_ref, wd_ref, x1_ref, g2_ref, fw_ref, o_ref, acc_ref, *, n_tiles, final_norm):
    f = pl.program_id(1)

    @pl.when(f == 0)
    def _():
        acc_ref[...] = jnp.zeros(acc_ref.shape, jnp.float32)

    h = h_ref[...]
    g = jnp.dot(h, wg_ref[...], preferred_element_type=jnp.float32)
    u = jnp.dot(h, wu_ref[...], preferred_element_type=jnp.float32)
    a = (g * jax.nn.sigmoid(g) * u).astype(h.dtype)
    acc_ref[...] += jnp.dot(a, wd_ref[...], preferred_element_type=jnp.float32)

    @pl.when(f == n_tiles - 1)
    def _():
        x2 = x1_ref[...] + g2_ref[...] * acc_ref[...]
        if final_norm:
            x2 = x2 * lax.rsqrt(jnp.mean(x2 * x2, axis=-1, keepdims=True) + NORM_EPS) * fw_ref[...]
        o_ref[...] = x2


def _ffn(h2, w_gate, w_up, w_down, x1, g2, final_w, *, final_norm, tm=512, tf=512):
    s, d = x1.shape
    dff = w_gate.shape[1]
    n_tiles = dff // tf
    kern = functools.partial(_ffn_kernel, n_tiles=n_tiles, final_norm=final_norm)
    row = lambda i, f: (0, 0)
    return pl.pallas_call(
        kern,
        grid=(s // tm, n_tiles),
        in_specs=[pl.BlockSpec((tm, d), lambda i, f: (i, 0)),
                  pl.BlockSpec((d, tf), lambda i, f: (0, f)),
                  pl.BlockSpec((d, tf), lambda i, f: (0, f)),
                  pl.BlockSpec((tf, d), lambda i, f: (f, 0)),
                  pl.BlockSpec((tm, d), lambda i, f: (i, 0)),
                  pl.BlockSpec((1, d), row), pl.BlockSpec((1, d), row)],
        out_specs=pl.BlockSpec((tm, d), lambda i, f: (i, 0)),
        out_shape=jax.ShapeDtypeStruct((s, d), jnp.float32),
        scratch_shapes=[pltpu.VMEM((tm, d), jnp.float32)],
        compiler_params=_cparams(("parallel", "arbitrary")),
        name="ffn",
    )(h2, w_gate, w_up, w_down, x1, g2, final_w)


def _rope_tables(s):
    n_rows = s // GRID_W
    axis_dim = HEAD_DIM // 2
    inv = ROPE_THETA ** (-jnp.arange(0, axis_dim, 2, dtype=jnp.float32) / axis_dim)
    lane = np.arange(HEAD_DIM)
    inv_lane = inv[(lane // 2) % (axis_dim // 2)]
    by_col = (lane >= axis_dim)[None, :]
    even = (lane % 2 == 0)[None, :]

    def parts(n, mine):
        ang = jnp.arange(n, dtype=jnp.float32)[:, None] * inv_lane[None]
        cos, sin = jnp.cos(ang), jnp.sin(ang)
        tabs = jnp.stack([cos, jnp.where(even, -sin, 0.0), jnp.where(even, 0.0, sin)])
        return jnp.where(mine[None], tabs, 0.0)

    return parts(n_rows, ~by_col), parts(GRID_W, by_col)


def kernel(x, c, w_ada, b_ada, norm1_w, w_in, q_norm_w, k_norm_w, nat_rpb, w_oa, w_ob, w_out, norm2_w,
           w_ffn_gate, w_ffn_up, w_ffn_down, final_w):
    b, s, d = x.shape
    assert b == 1
    depth = w_ada.shape[0]
    n_heads_a = w_oa.shape[1] // HEAD_DIM
    n_heads_b = w_ob.shape[1] // HEAD_DIM
    n_kv_a = n_heads_a // KV_GROUP
    splits = (n_heads_a * HEAD_DIM, n_kv_a * HEAD_DIM, n_kv_a * HEAD_DIM,
              n_heads_b * HEAD_DIM, n_heads_b * HEAD_DIM, n_heads_b * HEAD_DIM, d, d)
    offs = [0] + np.cumsum(splits)[:-1].tolist()
    qa_col, ka_col, va_col, qb_col, kb_col, vb_col, ga_col, gb_col = offs
    assert qa_col == 0

    bf = jnp.bfloat16
    assert va_col == ka_col + n_kv_a * HEAD_DIM
    rope_rows, rope_cols = _rope_tables(s)
    xs = x[0]
    c_col = c.reshape(d, 1)
    for l in range(depth):
        mod = _adaln(c_col, w_ada[l], b_ada[l][None, :])
        sh1, sc1, g1, sh2, sc2, g2 = [mod[:, i * d:(i + 1) * d] for i in range(6)]
        proj, vt = _inproj(xs, norm1_w[l][None, :], sc1, sh1, w_in[l].astype(bf),
                           q_norm_w[l][None, :], k_norm_w[l][None, :], rope_rows, rope_cols,
                           n_q_heads=n_heads_a, n_k_heads=n_kv_a, qb_cols=(qb_col, kb_col),
                           q_scale=HEAD_DIM ** -0.5 * LOG2E)
        ya = _gqa(proj, vt, n_q_heads=n_heads_a, n_kv_heads=n_kv_a, k_col=ka_col)
        yb = _natten(proj, nat_rpb[l], n_heads=n_heads_b,
                     q_col=qb_col, k_col=kb_col, v_col=vb_col)
        xs, h2 = _merge(ya, yb, proj, w_oa[l].astype(bf), w_ob[l].astype(bf), w_out[l].astype(bf),
                        xs, g1, norm2_w[l][None, :], sc2, sh2, ga_col=ga_col, gb_col=gb_col)
        xs = _ffn(h2, w_ffn_gate[l].astype(bf), w_ffn_up[l].astype(bf), w_ffn_down[l].astype(bf),
                  xs, g2, final_w[None, :], final_norm=(l == depth - 1))
    return xs[None]
```

```python
import functools
import math

import jax
import jax.numpy as jnp
import numpy as np
from jax import lax
from jax.experimental import pallas as pl
from jax.experimental.pallas import tpu as pltpu

HEAD_DIM = 128
KV_GROUP = 4
GRID_W = 64
NA_ROWS = 8
NA_COLS = 16
ROPE_THETA = 10000.0
NORM_EPS = 1e-6
NEG_BIG = -1e30
LOG2E = 1.4426950408889634
SUBLANES = 8
GQA_KEY_CHUNK = 512
GQA_ONES_ROWS = 16
GQA_STEPS_PER_ITER = 2

VMEM_LIMIT = 56 * 1024 * 1024

_NT = (((1,), (1,)), ((), ()))


def _cparams(sem):
    return pltpu.CompilerParams(dimension_semantics=sem, vmem_limit_bytes=VMEM_LIMIT)


def _adaln_kernel(c_ref, w_ref, b_ref, o_ref):
    c = c_ref[...]
    act = c * jax.nn.sigmoid(c)
    o_ref[...] = jnp.sum(act * w_ref[...], axis=0, keepdims=True) + b_ref[...]


def _adaln(c_col, w_ada, b_ada, tn=1024):
    d, n = w_ada.shape
    return pl.pallas_call(
        _adaln_kernel,
        grid=(n // tn,),
        in_specs=[pl.BlockSpec((d, 1), lambda j: (0, 0)),
                  pl.BlockSpec((d, tn), lambda j: (0, j)),
                  pl.BlockSpec((1, tn), lambda j: (0, j))],
        out_specs=pl.BlockSpec((1, tn), lambda j: (0, j)),
        out_shape=jax.ShapeDtypeStruct((1, n), jnp.float32),
        compiler_params=_cparams(("arbitrary",)),
        name="adaln",
    )(c_col, w_ada, b_ada)


def _rms_modulate(x, w, sc, sh):
    rstd = lax.rsqrt(jnp.mean(x * x, axis=-1, keepdims=True) + NORM_EPS)
    return (x * rstd) * (w * (1.0 + sc)) + sh


def _head_norm_rope(t, w, cos, sin_e, sin_o, scale):
    t = t * lax.rsqrt(jnp.mean(t * t, axis=-1, keepdims=True) + NORM_EPS) * w
    nxt = pltpu.roll(t, HEAD_DIM - 1, 1)
    prv = pltpu.roll(t, 1, 1)
    return (t * cos + nxt * sin_e + prv * sin_o) * scale


def _inproj_kernel(x_ref, n1_ref, sc_ref, sh_ref, w_ref, qw_ref, kw_ref, rt_ref, ct_ref,
                   o_ref, vt_ref, h_ref, rope_ref, *, n_q_tiles, heads_per_tile, n_k_heads, qb_tiles, q_scale):
    j = pl.program_id(1)
    tm = x_ref.shape[0]

    @pl.when(j == 0)
    def _():
        h_ref[...] = _rms_modulate(x_ref[...], n1_ref[...], sc_ref[...], sh_ref[...]).astype(h_ref.dtype)
        for k in range(3):
            for r in range(tm // GRID_W):
                rope_ref[k, r * GRID_W:(r + 1) * GRID_W, :] = rt_ref[k, r:r + 1, :] + ct_ref[k]

    acc = jnp.dot(h_ref[...], w_ref[...], preferred_element_type=jnp.float32)

    def rope_heads(n_heads, w, scale):
        cos, se, so = rope_ref[0], rope_ref[1], rope_ref[2]
        for hh in range(heads_per_tile):
            sl = slice(hh * HEAD_DIM, (hh + 1) * HEAD_DIM)
            t = acc[:, sl]
            if hh < n_heads:
                t = _head_norm_rope(t, w, cos, se, so, scale)
            o_ref[:, sl] = t.astype(o_ref.dtype)

    @pl.when(j < n_q_tiles)
    def _():
        rope_heads(heads_per_tile, qw_ref[...], q_scale)

    @pl.when(j == n_q_tiles)
    def _():
        rope_heads(n_k_heads, kw_ref[...], 1.0)
        n_v, n_chunks, _, tk = vt_ref.shape
        for vh in range(n_v):
            col = (n_k_heads + vh) * HEAD_DIM
            for cc in range(n_chunks):
                v = acc[cc * tk:(cc + 1) * tk, col:col + HEAD_DIM]
                vt_ref[vh, cc, :HEAD_DIM, :] = v.T.astype(vt_ref.dtype)
                vt_ref[vh, cc, HEAD_DIM:, :] = jnp.ones((GQA_ONES_ROWS, tk), vt_ref.dtype)

    in_qb = (j >= qb_tiles[0]) & (j < qb_tiles[1])

    @pl.when(in_qb)
    def _():
        o_ref[...] = (acc * q_scale).astype(o_ref.dtype)

    @pl.when((j > n_q_tiles) & jnp.logical_not(in_qb))
    def _():
        o_ref[...] = acc.astype(o_ref.dtype)


def _inproj(x, n1, sc1, sh1, w_in, qw, kw, rope_rows, rope_cols, *, n_q_heads, n_k_heads, qb_cols, q_scale,
            tm=1024, tn=512, tk=GQA_KEY_CHUNK):
    s, d = x.shape
    n = w_in.shape[1]
    heads_per_tile = tn // HEAD_DIM
    assert n_q_heads % heads_per_tile == 0 and 2 * n_k_heads == heads_per_tile
    assert qb_cols[0] % tn == 0 and qb_cols[1] % tn == 0 and qb_cols[0] // tn > n_q_heads // heads_per_tile
    assert tm % tk == 0 and tm % GRID_W == 0 and (tm // GRID_W) % SUBLANES == 0
    kern = functools.partial(_inproj_kernel, n_q_tiles=n_q_heads // heads_per_tile,
                             heads_per_tile=heads_per_tile, n_k_heads=n_k_heads,
                             qb_tiles=(qb_cols[0] // tn, qb_cols[1] // tn), q_scale=q_scale)
    row = lambda i, j: (0, 0)
    vt_rows = HEAD_DIM + GQA_ONES_ROWS
    return pl.pallas_call(
        kern,
        grid=(s // tm, n // tn),
        in_specs=[pl.BlockSpec((tm, d), lambda i, j: (i, 0)),
                  pl.BlockSpec((1, d), row), pl.BlockSpec((1, d), row), pl.BlockSpec((1, d), row),
                  pl.BlockSpec((d, tn), lambda i, j: (0, j)),
                  pl.BlockSpec((1, HEAD_DIM), row), pl.BlockSpec((1, HEAD_DIM), row),
                  pl.BlockSpec((3, tm // GRID_W, HEAD_DIM), lambda i, j: (0, i, 0)),
                  pl.BlockSpec((3, GRID_W, HEAD_DIM), lambda i, j: (0, 0, 0))],
        out_specs=[pl.BlockSpec((tm, tn), lambda i, j: (i, j)),
                   pl.BlockSpec((n_k_heads, tm // tk, vt_rows, tk), lambda i, j: (0, i, 0, 0))],
        out_shape=[jax.ShapeDtypeStruct((s, n), jnp.bfloat16),
                   jax.ShapeDtypeStruct((n_k_heads, s // tk, vt_rows, tk), jnp.bfloat16)],
        scratch_shapes=[pltpu.VMEM((tm, d), jnp.bfloat16),
                        pltpu.VMEM((3, tm, HEAD_DIM), jnp.float32)],
        compiler_params=_cparams(("parallel", "arbitrary")),
        name="inproj",
    )(x, n1, sc1, sh1, w_in, qw, kw, rope_rows, rope_cols)


def _sublane_allreduce(x, op):
    shift = SUBLANES // 2
    while shift:
        x = op(x, pltpu.roll(x, shift, 0))
        shift //= 2
    return x


def _gqa_kernel(q_ref, qn_ref, k_ref, vt_ref, o_ref, qt_ref, s0_ref, s1_ref, p0_ref, p1_ref, x0_ref, x1_ref,
                a0_ref, a1_ref, m_ref, acc_ref, *, tq, tk, n_chunks):
    s_refs, p_refs, x_refs, a_refs = (s0_ref, s1_ref), (p0_ref, p1_ref), (x0_ref, x1_ref), (a0_ref, a1_ref)
    nq = KV_GROUP * tq
    acc_rows = HEAD_DIM + GQA_ONES_ROWS

    def scores(c, slot):
        k = k_ref[pl.ds(pl.multiple_of(c * tk, tk), tk), :]
        s = jnp.dot(k, qt_ref[...], preferred_element_type=jnp.float32)
        s_refs[slot][...] = s
        x_refs[slot][...] = jnp.max(s.reshape(tk // SUBLANES, SUBLANES, nq), axis=0)

    def softmax(slot):
        m_old = m_ref[...]
        m_new = jnp.maximum(m_old, _sublane_allreduce(x_refs[slot][...], jnp.maximum))
        a_refs[slot][...] = jnp.exp2(m_old - m_new)
        m_ref[...] = m_new
        s = s_refs[slot][...].reshape(tk // SUBLANES, SUBLANES, nq)
        p_refs[slot][...] = jnp.exp2(s - m_new[None]).reshape(tk, nq).astype(p_refs[slot].dtype)

    def weighted_values(c, slot):
        acc = acc_ref[...].reshape(acc_rows // SUBLANES, SUBLANES, nq) * a_refs[slot][...][None]
        acc_ref[...] = acc.reshape(acc_rows, nq) + jnp.dot(
            vt_ref[c], p_refs[slot][...], preferred_element_type=jnp.float32)

    def load_queries(src_ref):
        for g in range(KV_GROUP):
            qg = src_ref[:, g * HEAD_DIM:(g + 1) * HEAD_DIM].astype(jnp.float32)
            qt_ref[:, g * tq:(g + 1) * tq] = qg.T.astype(qt_ref.dtype)

    def reset_and_first_softmax():
        m_ref[...] = jnp.full(m_ref.shape, -jnp.inf, jnp.float32)
        acc_ref[...] = jnp.zeros(acc_ref.shape, jnp.float32)
        softmax(0)

    unroll = GQA_STEPS_PER_ITER
    assert unroll % 2 == 0 and (n_chunks - 2) % unroll == 0

    @pl.when(pl.program_id(1) == 0)
    def _():
        load_queries(q_ref)
        scores(0, 0)
        scores(1, 1)
        reset_and_first_softmax()

    def body(i, carry):
        for u in range(unroll):
            t = unroll * i + 1 + u
            cur = (1 + u) % 2
            scores(t + 1, 1 - cur)
            softmax(cur)
            weighted_values(t - 1, 1 - cur)
        return carry

    lax.fori_loop(0, (n_chunks - 2) // unroll, body, 0)
    load_queries(qn_ref)
    scores(0, 0)
    softmax(1)
    weighted_values(n_chunks - 2, 0)
    scores(1, 1)
    weighted_values(n_chunks - 1, 1)
    acc = acc_ref[...]
    inv_l = 1.0 / acc[HEAD_DIM:HEAD_DIM + SUBLANES, :]
    out_t = (acc[:HEAD_DIM, :].reshape(HEAD_DIM // SUBLANES, SUBLANES, nq) * inv_l[None]).reshape(HEAD_DIM, nq)
    for g in range(KV_GROUP):
        o_ref[:, g * HEAD_DIM:(g + 1) * HEAD_DIM] = out_t[:, g * tq:(g + 1) * tq].T.astype(o_ref.dtype)
    reset_and_first_softmax()


def _gqa(proj, vt, *, n_q_heads, n_kv_heads, k_col, tq=1024):
    s = proj.shape[0]
    _, n_chunks, acc_rows, tk = vt.shape
    gw = KV_GROUP * HEAD_DIM
    nq = KV_GROUP * tq
    n_blocks = s // tq
    kern = functools.partial(_gqa_kernel, tq=tq, tk=tk, n_chunks=n_chunks)
    stat = pltpu.VMEM((SUBLANES, nq), jnp.float32)
    return pl.pallas_call(
        kern,
        grid=(n_kv_heads, n_blocks),
        in_specs=[pl.BlockSpec((tq, gw), lambda h, i: (i, h)),
                  pl.BlockSpec((tq, gw), lambda h, i: (jnp.minimum(i + 1, n_blocks - 1), h)),
                  pl.BlockSpec((s, HEAD_DIM), lambda h, i: (0, k_col // HEAD_DIM + h),
                               pipeline_mode=pl.Buffered(1)),
                  pl.BlockSpec((None, n_chunks, acc_rows, tk), lambda h, i: (h, 0, 0, 0),
                               pipeline_mode=pl.Buffered(1))],
        out_specs=pl.BlockSpec((tq, gw), lambda h, i: (i, h)),
        out_shape=jax.ShapeDtypeStruct((s, n_q_heads * HEAD_DIM), jnp.bfloat16),
        scratch_shapes=[pltpu.VMEM((HEAD_DIM, nq), jnp.bfloat16),
                        pltpu.VMEM((tk, nq), jnp.float32), pltpu.VMEM((tk, nq), jnp.float32),
                        pltpu.VMEM((tk, nq), jnp.bfloat16), pltpu.VMEM((tk, nq), jnp.bfloat16),
                        stat, stat, stat, stat, stat,
                        pltpu.VMEM((acc_rows, nq), jnp.float32)],
        compiler_params=_cparams(("arbitrary", "arbitrary")),
        name="gqa",
    )(proj, proj, proj, vt)


NA_GROUP = 4
NA_WIN_ROWS = NA_GROUP + NA_ROWS


def _na_window_start(r0, n_rows):
    return jnp.clip(r0 - NA_ROWS // 2, 0, n_rows - NA_WIN_ROWS)


def _na_bias_rows(n_rows):
    rel = []
    for r0 in (0, NA_WIN_ROWS, n_rows - NA_GROUP):
        ws = int(np.clip(r0 - NA_ROWS // 2, 0, n_rows - NA_WIN_ROWS))
        per_row = []
        for i in range(NA_GROUP):
            rs = int(np.clip(r0 + i - NA_ROWS // 2, 0, n_rows - NA_ROWS))
            per_row.append([ws + wr - (r0 + i) + NA_ROWS - 1 if rs <= ws + wr < rs + NA_ROWS else None
                            for wr in range(NA_WIN_ROWS)])
        rel.append(per_row)
    return rel


def _na_build_bias(rpb_ref, b_ref, n_rows):
    shape = (GRID_W, 2 * GRID_W)
    lane = lax.broadcasted_iota(jnp.int32, shape, 1)
    c = lax.broadcasted_iota(jnp.int32, shape, 0)
    left = lane < GRID_W
    kc = jnp.where(left, lane, lane - GRID_W)
    cs = jnp.clip(c - NA_COLS // 2, 0, GRID_W - NA_COLS)
    col_ok = (kc >= cs) & (kc < cs + NA_COLS)
    neg = jnp.full(shape, NEG_BIG, jnp.float32)
    n_rel = 2 * NA_ROWS - 1
    tiles = []
    for dr in range(n_rel):
        row = jnp.broadcast_to(rpb_ref[dr:dr + 1, :], shape) * LOG2E
        lo = pltpu.roll(row, 2 * GRID_W - (NA_COLS - 1), 1, stride=1, stride_axis=0)
        hi = pltpu.roll(row, GRID_W - (NA_COLS - 1), 1, stride=1, stride_axis=0)
        tiles.append(jnp.where(col_ok, jnp.where(left, lo, hi), neg))
    rel = _na_bias_rows(n_rows)
    for v in range(3):
        for i in range(NA_GROUP):
            for j in range(NA_WIN_ROWS // 2):
                a, b = rel[v][i][2 * j], rel[v][i][2 * j + 1]
                pair = jnp.where(left, neg if a is None else tiles[a], neg if b is None else tiles[b])
                b_ref[v, i * GRID_W:(i + 1) * GRID_W, j * 2 * GRID_W:(j + 1) * 2 * GRID_W] = pair


def _natten_kernel(q_ref, k_ref, v_ref, rpb_ref, *rest, groups_per_step, n_rows, cast_periods):
    n_cast = len(cast_periods)
    cast_src, o_ref, cast_dst, b_ref = rest[:n_cast], rest[n_cast], rest[n_cast + 1:-1], rest[-1]
    gq = NA_GROUP * GRID_W
    win = NA_WIN_ROWS * GRID_W
    step = pl.program_id(0) * pl.num_programs(1) + pl.program_id(1)
    for src, dst, period in zip(cast_src, cast_dst, cast_periods):
        @pl.when(step % period == 0)
        def _():
            if len(dst.shape) == 3:
                tn = dst.shape[2]
                for n in range(dst.shape[0]):
                    dst[n] = src[:, n * tn:(n + 1) * tn].astype(dst.dtype)
            else:
                dst[...] = src[...].astype(dst.dtype)

    @pl.when(pl.program_id(1) == 0)
    def _():
        _na_build_bias(rpb_ref, b_ref, n_rows)

    for g in range(groups_per_step):
        r0 = (pl.program_id(1) * groups_per_step + g) * NA_GROUP
        variant = jnp.where(r0 == 0, 0, jnp.where(r0 == n_rows - NA_GROUP, 2, 1))
        k0 = pl.multiple_of(_na_window_start(r0, n_rows) * GRID_W, GRID_W)
        q = q_ref[g * gq:(g + 1) * gq, :]
        kw = k_ref[pl.ds(k0, win), :]
        vw = v_ref[pl.ds(k0, win), :]
        s = lax.dot_general(q, kw, _NT, preferred_element_type=jnp.float32) + b_ref[variant]
        p = jnp.exp2(s - jnp.max(s, axis=-1, keepdims=True))
        l = jnp.sum(p, axis=-1, keepdims=True)
        o = jnp.dot(p.astype(vw.dtype), vw, preferred_element_type=jnp.float32) / l
        o_ref[g * gq:(g + 1) * gq, :] = o.astype(o_ref.dtype)


def _cast_plan(w, tile_cols, n_steps, step_of):
    rows, cols = w.shape
    n_blocks = next(nb for nb in range(n_steps, 0, -1)
                    if n_steps % nb == 0 and rows % nb == 0 and (rows // nb) % (2 * SUBLANES) == 0)
    rb, period = rows // n_blocks, n_steps // n_blocks
    src = pl.BlockSpec((rb, cols), lambda h, i: (step_of(h, i) // period, 0))
    if tile_cols is None:
        dst = pl.BlockSpec((rb, cols), lambda h, i: (step_of(h, i) // period, 0))
        shape = jax.ShapeDtypeStruct((rows, cols), jnp.bfloat16)
    else:
        n_tiles = cols // tile_cols
        dst = pl.BlockSpec((n_tiles, rb, tile_cols), lambda h, i: (0, step_of(h, i) // period, 0))
        shape = jax.ShapeDtypeStruct((n_tiles, rows, tile_cols), jnp.bfloat16)
    return src, dst, shape, period


def _natten(proj, rpb, casts, *, n_heads, q_col, k_col, v_col, groups_per_step=4):
    s = proj.shape[0]
    n_rows = s // GRID_W
    tq = groups_per_step * NA_GROUP * GRID_W
    gq, win = NA_GROUP * GRID_W, NA_WIN_ROWS * GRID_W
    assert n_rows % (groups_per_step * NA_GROUP) == 0 and n_rows >= 2 * NA_WIN_ROWS
    assert NA_WIN_ROWS % 2 == 0 and 2 * GRID_W == HEAD_DIM and rpb.shape[2] <= GRID_W
    rel_rows = -(-rpb.shape[1] // SUBLANES) * SUBLANES
    rpb_rows = jnp.pad(rpb, ((0, 0), (0, rel_rows - rpb.shape[1]), (0, 2 * GRID_W - rpb.shape[2])))
    n_i = s // tq
    plans = [_cast_plan(w, tile_cols, n_heads * n_i, lambda h, i: h * n_i + i) for w, tile_cols in casts]
    kern = functools.partial(_natten_kernel, groups_per_step=groups_per_step, n_rows=n_rows,
                             cast_periods=tuple(p[3] for p in plans))
    out = pl.pallas_call(
        kern,
        grid=(n_heads, n_i),
        in_specs=[pl.BlockSpec((tq, HEAD_DIM), lambda h, i: (i, q_col // HEAD_DIM + h)),
                  pl.BlockSpec((s, HEAD_DIM), lambda h, i: (0, k_col // HEAD_DIM + h)),
                  pl.BlockSpec((s, HEAD_DIM), lambda h, i: (0, v_col // HEAD_DIM + h)),
                  pl.BlockSpec((None, rel_rows, 2 * GRID_W), lambda h, i: (h, 0, 0))] + [p[0] for p in plans],
        out_specs=[pl.BlockSpec((tq, HEAD_DIM), lambda h, i: (i, h))] + [p[1] for p in plans],
        out_shape=[jax.ShapeDtypeStruct((s, n_heads * HEAD_DIM), jnp.bfloat16)] + [p[2] for p in plans],
        scratch_shapes=[pltpu.VMEM((3, gq, win), jnp.float32)],
        compiler_params=_cparams(("arbitrary", "arbitrary")),
        name="natten",
    )(proj, proj, proj, rpb_rows, *[w for w, _ in casts])
    return out[0], out[1:]


def _merge_kernel(ya_ref, yb_ref, ga_ref, gb_ref, woa_ref, wob_ref, wout_ref, x_ref, g1_ref, n2_ref, sc_ref,
                  sh_ref, x1_ref, h2_ref, m_ref, *, tn, n_tiles):
    n = pl.program_id(1)
    a = jnp.dot(ya_ref[...], woa_ref[n], preferred_element_type=jnp.float32)
    b = jnp.dot(yb_ref[...], wob_ref[n], preferred_element_type=jnp.float32)
    m = (jax.nn.sigmoid(ga_ref[...].astype(jnp.float32)) * a
         + jax.nn.sigmoid(gb_ref[...].astype(jnp.float32)) * b)
    col = pl.multiple_of(n * tn, tn)
    m_ref[:, pl.ds(col, tn)] = m.astype(m_ref.dtype)

    @pl.when(n == n_tiles - 1)
    def _():
        y = jnp.dot(m_ref[...], wout_ref[...], preferred_element_type=jnp.float32)
        x1 = x_ref[...] + g1_ref[...] * y
        x1_ref[...] = x1
        h2_ref[...] = _rms_modulate(x1, n2_ref[...], sc_ref[...], sh_ref[...]).astype(h2_ref.dtype)


MERGE_TILE = 512


def _merge(ya, yb, proj, woa_tiles, wob_tiles, w_out, x, g1, n2, sc2, sh2, *, ga_col, gb_col, tm=512):
    s, d = x.shape
    da = ya.shape[1]
    n_tiles, _, tn = woa_tiles.shape
    kern = functools.partial(_merge_kernel, tn=tn, n_tiles=n_tiles)
    row = lambda i, n: (0, 0)
    tiles = pl.BlockSpec((n_tiles, da, tn), lambda i, n: (0, 0, 0), pipeline_mode=pl.Buffered(1))
    return pl.pallas_call(
        kern,
        grid=(s // tm, n_tiles),
        in_specs=[pl.BlockSpec((tm, da), lambda i, n: (i, 0)),
                  pl.BlockSpec((tm, da), lambda i, n: (i, 0)),
                  pl.BlockSpec((tm, tn), lambda i, n: (i, ga_col // tn + n)),
                  pl.BlockSpec((tm, tn), lambda i, n: (i, gb_col // tn + n)),
                  tiles, tiles,
                  pl.BlockSpec((d, d), row, pipeline_mode=pl.Buffered(1)),
                  pl.BlockSpec((tm, d), lambda i, n: (i, 0)),
                  pl.BlockSpec((1, d), row), pl.BlockSpec((1, d), row),
                  pl.BlockSpec((1, d), row), pl.BlockSpec((1, d), row)],
        out_specs=[pl.BlockSpec((tm, d), lambda i, n: (i, 0)),
                   pl.BlockSpec((tm, d), lambda i, n: (i, 0))],
        out_shape=[jax.ShapeDtypeStruct((s, d), jnp.float32),
                   jax.ShapeDtypeStruct((s, d), jnp.bfloat16)],
        scratch_shapes=[pltpu.VMEM((tm, d), jnp.bfloat16)],
        compiler_params=_cparams(("parallel", "arbitrary")),
        name="merge",
    )(ya, yb, proj, proj, woa_tiles, wob_tiles, w_out, x, g1, n2, sc2, sh2)


def _ffn_kernel(h_ref, wg_ref, wu_ref, wd_ref, x1_ref, g2_ref, fw_ref, o_ref, acc_ref, *, n_tiles, final_norm):
    f = pl.program_id(1)

    @pl.when(f == 0)
    def _():
        acc_ref[...] = jnp.zeros(acc_ref.shape, jnp.float32)

    h = h_ref[...]
    g = jnp.dot(h, wg_ref[...], preferred_element_type=jnp.float32)
    u = jnp.dot(h, wu_ref[...], preferred_element_type=jnp.float32)
    a = (g * jax.nn.sigmoid(g) * u).astype(h.dtype)
    acc_ref[...] += jnp.dot(a, wd_ref[...], preferred_element_type=jnp.float32)

    @pl.when(f == n_tiles - 1)
    def _():
        x2 = x1_ref[...] + g2_ref[...] * acc_ref[...]
        if final_norm:
            x2 = x2 * lax.rsqrt(jnp.mean(x2 * x2, axis=-1, keepdims=True) + NORM_EPS) * fw_ref[...]
        o_ref[...] = x2


def _ffn(h2, w_gate, w_up, w_down, x1, g2, final_w, *, final_norm, tm=512, tf=512):
    s, d = x1.shape
    dff = w_gate.shape[1]
    n_tiles = dff // tf
    kern = functools.partial(_ffn_kernel, n_tiles=n_tiles, final_norm=final_norm)
    row = lambda i, f: (0, 0)
    return pl.pallas_call(
        kern,
        grid=(s // tm, n_tiles),
        in_specs=[pl.BlockSpec((tm, d), lambda i, f: (i, 0)),
                  pl.BlockSpec((d, tf), lambda i, f: (0, f)),
                  pl.BlockSpec((d, tf), lambda i, f: (0, f)),
                  pl.BlockSpec((tf, d), lambda i, f: (f, 0)),
                  pl.BlockSpec((tm, d), lambda i, f: (i, 0)),
                  pl.BlockSpec((1, d), row), pl.BlockSpec((1, d), row)],
        out_specs=pl.BlockSpec((tm, d), lambda i, f: (i, 0)),
        out_shape=jax.ShapeDtypeStruct((s, d), jnp.float32),
        scratch_shapes=[pltpu.VMEM((tm, d), jnp.float32)],
        compiler_params=_cparams(("parallel", "arbitrary")),
        name="ffn",
    )(h2, w_gate, w_up, w_down, x1, g2, final_w)


def _rope_tables(s):
    n_rows = s // GRID_W
    axis_dim = HEAD_DIM // 2
    inv = ROPE_THETA ** (-jnp.arange(0, axis_dim, 2, dtype=jnp.float32) / axis_dim)
    lane = np.arange(HEAD_DIM)
    inv_lane = inv[(lane // 2) % (axis_dim // 2)]
    by_col = (lane >= axis_dim)[None, :]
    even = (lane % 2 == 0)[None, :]

    def parts(n, mine):
        ang = jnp.arange(n, dtype=jnp.float32)[:, None] * inv_lane[None]
        cos, sin = jnp.cos(ang), jnp.sin(ang)
        tabs = jnp.stack([cos, jnp.where(even, -sin, 0.0), jnp.where(even, 0.0, sin)])
        return jnp.where(mine[None], tabs, 0.0)

    return parts(n_rows, ~by_col), parts(GRID_W, by_col)


def kernel(x, c, w_ada, b_ada, norm1_w, w_in, q_norm_w, k_norm_w, nat_rpb, w_oa, w_ob, w_out, norm2_w,
           w_ffn_gate, w_ffn_up, w_ffn_down, final_w):
    b, s, d = x.shape
    assert b == 1
    depth = w_ada.shape[0]
    n_heads_a = w_oa.shape[1] // HEAD_DIM
    n_heads_b = w_ob.shape[1] // HEAD_DIM
    n_kv_a = n_heads_a // KV_GROUP
    splits = (n_heads_a * HEAD_DIM, n_kv_a * HEAD_DIM, n_kv_a * HEAD_DIM,
              n_heads_b * HEAD_DIM, n_heads_b * HEAD_DIM, n_heads_b * HEAD_DIM, d, d)
    offs = [0] + np.cumsum(splits)[:-1].tolist()
    qa_col, ka_col, va_col, qb_col, kb_col, vb_col, ga_col, gb_col = offs
    assert qa_col == 0

    bf = jnp.bfloat16
    assert va_col == ka_col + n_kv_a * HEAD_DIM
    rope_rows, rope_cols = _rope_tables(s)
    xs = x[0]
    c_col = c.reshape(d, 1)
    for l in range(depth):
        mod = _adaln(c_col, w_ada[l], b_ada[l][None, :])
        sh1, sc1, g1, sh2, sc2, g2 = [mod[:, i * d:(i + 1) * d] for i in range(6)]
        proj, vt = _inproj(xs, norm1_w[l][None, :], sc1, sh1, w_in[l].astype(bf),
                           q_norm_w[l][None, :], k_norm_w[l][None, :], rope_rows, rope_cols,
                           n_q_heads=n_heads_a, n_k_heads=n_kv_a, qb_cols=(qb_col, kb_col),
                           q_scale=HEAD_DIM ** -0.5 * LOG2E)
        ya = _gqa(proj, vt, n_q_heads=n_heads_a, n_kv_heads=n_kv_a, k_col=ka_col)
        yb, (w_gate, w_up, w_down, w_o, woa_tiles, wob_tiles) = _natten(
            proj, nat_rpb[l],
            [(w_ffn_gate[l], None), (w_ffn_up[l], None), (w_ffn_down[l], None), (w_out[l], None),
             (w_oa[l], MERGE_TILE), (w_ob[l], MERGE_TILE)],
            n_heads=n_heads_b, q_col=qb_col, k_col=kb_col, v_col=vb_col)
        xs, h2 = _merge(ya, yb, proj, woa_tiles, wob_tiles, w_o, xs, g1, norm2_w[l][None, :], sc2, sh2,
                        ga_col=ga_col, gb_col=gb_col)
        xs = _ffn(h2, w_gate, w_up, w_down, xs, g2, final_w[None, :], final_norm=(l == depth - 1))
    return xs[None]
```

```python
import functools
import math

import jax
import jax.numpy as jnp
import numpy as np
from jax import lax
from jax.experimental import pallas as pl
from jax.experimental.pallas import tpu as pltpu

HEAD_DIM = 128
KV_GROUP = 4
GRID_W = 64
NA_ROWS = 8
NA_COLS = 16
ROPE_THETA = 10000.0
NORM_EPS = 1e-6
NEG_BIG = -1e30
LOG2E = 1.4426950408889634
SUBLANES = 8
GQA_KEY_CHUNK = 512
GQA_ONES_ROWS = 16
GQA_STEPS_PER_ITER = 2

VMEM_LIMIT = 56 * 1024 * 1024

_NT = (((1,), (1,)), ((), ()))


def _cparams(sem):
    return pltpu.CompilerParams(dimension_semantics=sem, vmem_limit_bytes=VMEM_LIMIT)


def _adaln_kernel(c_ref, w_ref, b_ref, o_ref):
    c = c_ref[...]
    act = c * jax.nn.sigmoid(c)
    o_ref[...] = jnp.sum(act * w_ref[...], axis=0, keepdims=True) + b_ref[...]


def _adaln(c_col, w_ada, b_ada, tn=1024):
    d, n = w_ada.shape
    return pl.pallas_call(
        _adaln_kernel,
        grid=(n // tn,),
        in_specs=[pl.BlockSpec((d, 1), lambda j: (0, 0)),
                  pl.BlockSpec((d, tn), lambda j: (0, j)),
                  pl.BlockSpec((1, tn), lambda j: (0, j))],
        out_specs=pl.BlockSpec((1, tn), lambda j: (0, j)),
        out_shape=jax.ShapeDtypeStruct((1, n), jnp.float32),
        compiler_params=_cparams(("arbitrary",)),
        name="adaln",
    )(c_col, w_ada, b_ada)


def _rms_modulate(x, w, sc, sh):
    rstd = lax.rsqrt(jnp.mean(x * x, axis=-1, keepdims=True) + NORM_EPS)
    return (x * rstd) * (w * (1.0 + sc)) + sh


def _head_norm_rope(t, w, cos, sin_e, sin_o, scale):
    t = t * lax.rsqrt(jnp.mean(t * t, axis=-1, keepdims=True) + NORM_EPS) * w
    nxt = pltpu.roll(t, HEAD_DIM - 1, 1)
    prv = pltpu.roll(t, 1, 1)
    return (t * cos + nxt * sin_e + prv * sin_o) * scale


def _inproj_kernel(x_ref, n1_ref, sc_ref, sh_ref, w_ref, qw_ref, kw_ref, rt_ref, ct_ref,
                   o_ref, vt_ref, h_ref, rope_ref, *, n_q_tiles, heads_per_tile, n_k_heads, qb_tiles, q_scale):
    j = pl.program_id(1)
    tm = x_ref.shape[0]

    @pl.when(j == 0)
    def _():
        h_ref[...] = _rms_modulate(x_ref[...], n1_ref[...], sc_ref[...], sh_ref[...]).astype(h_ref.dtype)
        for k in range(3):
            for r in range(tm // GRID_W):
                rope_ref[k, r * GRID_W:(r + 1) * GRID_W, :] = rt_ref[k, r:r + 1, :] + ct_ref[k]

    acc = jnp.dot(h_ref[...], w_ref[...], preferred_element_type=jnp.float32)

    def rope_heads(n_heads, w, scale):
        cos, se, so = rope_ref[0], rope_ref[1], rope_ref[2]
        for hh in range(heads_per_tile):
            sl = slice(hh * HEAD_DIM, (hh + 1) * HEAD_DIM)
            t = acc[:, sl]
            if hh < n_heads:
                t = _head_norm_rope(t, w, cos, se, so, scale)
            o_ref[:, sl] = t.astype(o_ref.dtype)

    @pl.when(j < n_q_tiles)
    def _():
        rope_heads(heads_per_tile, qw_ref[...], q_scale)

    @pl.when(j == n_q_tiles)
    def _():
        rope_heads(n_k_heads, kw_ref[...], 1.0)
        n_v, n_chunks, _, tk = vt_ref.shape
        for vh in range(n_v):
            col = (n_k_heads + vh) * HEAD_DIM
            for cc in range(n_chunks):
                v = acc[cc * tk:(cc + 1) * tk, col:col + HEAD_DIM]
                vt_ref[vh, cc, :HEAD_DIM, :] = v.T.astype(vt_ref.dtype)
                vt_ref[vh, cc, HEAD_DIM:, :] = jnp.ones((GQA_ONES_ROWS, tk), vt_ref.dtype)

    in_qb = (j >= qb_tiles[0]) & (j < qb_tiles[1])

    @pl.when(in_qb)
    def _():
        o_ref[...] = (acc * q_scale).astype(o_ref.dtype)

    @pl.when((j > n_q_tiles) & jnp.logical_not(in_qb))
    def _():
        o_ref[...] = acc.astype(o_ref.dtype)


def _inproj(x, n1, sc1, sh1, w_in, qw, kw, rope_rows, rope_cols, *, n_q_heads, n_k_heads, qb_cols, q_scale,
            tm=1024, tn=512, tk=GQA_KEY_CHUNK):
    s, d = x.shape
    n = w_in.shape[1]
    heads_per_tile = tn // HEAD_DIM
    assert n_q_heads % heads_per_tile == 0 and 2 * n_k_heads == heads_per_tile
    assert qb_cols[0] % tn == 0 and qb_cols[1] % tn == 0 and qb_cols[0] // tn > n_q_heads // heads_per_tile
    assert tm % tk == 0 and tm % GRID_W == 0 and (tm // GRID_W) % SUBLANES == 0
    kern = functools.partial(_inproj_kernel, n_q_tiles=n_q_heads // heads_per_tile,
                             heads_per_tile=heads_per_tile, n_k_heads=n_k_heads,
                             qb_tiles=(qb_cols[0] // tn, qb_cols[1] // tn), q_scale=q_scale)
    row = lambda i, j: (0, 0)
    vt_rows = HEAD_DIM + GQA_ONES_ROWS
    return pl.pallas_call(
        kern,
        grid=(s // tm, n // tn),
        in_specs=[pl.BlockSpec((tm, d), lambda i, j: (i, 0)),
                  pl.BlockSpec((1, d), row), pl.BlockSpec((1, d), row), pl.BlockSpec((1, d), row),
                  pl.BlockSpec((d, tn), lambda i, j: (0, j)),
                  pl.BlockSpec((1, HEAD_DIM), row), pl.BlockSpec((1, HEAD_DIM), row),
                  pl.BlockSpec((3, tm // GRID_W, HEAD_DIM), lambda i, j: (0, i, 0)),
                  pl.BlockSpec((3, GRID_W, HEAD_DIM), lambda i, j: (0, 0, 0))],
        out_specs=[pl.BlockSpec((tm, tn), lambda i, j: (i, j)),
                   pl.BlockSpec((n_k_heads, tm // tk, vt_rows, tk), lambda i, j: (0, i, 0, 0))],
        out_shape=[jax.ShapeDtypeStruct((s, n), jnp.bfloat16),
                   jax.ShapeDtypeStruct((n_k_heads, s // tk, vt_rows, tk), jnp.bfloat16)],
        scratch_shapes=[pltpu.VMEM((tm, d), jnp.bfloat16),
                        pltpu.VMEM((3, tm, HEAD_DIM), jnp.float32)],
        compiler_params=_cparams(("parallel", "arbitrary")),
        name="inproj",
    )(x, n1, sc1, sh1, w_in, qw, kw, rope_rows, rope_cols)


def _sublane_allreduce(x, op):
    shift = SUBLANES // 2
    while shift:
        x = op(x, pltpu.roll(x, shift, 0))
        shift //= 2
    return x


def _gqa_kernel(q_ref, qn_ref, k_ref, vt_ref, o_ref, qt_ref, s0_ref, s1_ref, p0_ref, p1_ref, x0_ref, x1_ref,
                a0_ref, a1_ref, m_ref, acc_ref, *, tq, tk, n_chunks):
    s_refs, p_refs, x_refs, a_refs = (s0_ref, s1_ref), (p0_ref, p1_ref), (x0_ref, x1_ref), (a0_ref, a1_ref)
    nq = KV_GROUP * tq
    acc_rows = HEAD_DIM + GQA_ONES_ROWS

    def scores(c, slot):
        k = k_ref[pl.ds(pl.multiple_of(c * tk, tk), tk), :]
        s = jnp.dot(k, qt_ref[...], preferred_element_type=jnp.float32)
        s_refs[slot][...] = s
        x_refs[slot][...] = jnp.max(s.reshape(tk // SUBLANES, SUBLANES, nq), axis=0)

    def softmax(slot):
        m_old = m_ref[...]
        m_new = jnp.maximum(m_old, _sublane_allreduce(x_refs[slot][...], jnp.maximum))
        a_refs[slot][...] = jnp.exp2(m_old - m_new)
        m_ref[...] = m_new
        s = s_refs[slot][...].reshape(tk // SUBLANES, SUBLANES, nq)
        p_refs[slot][...] = jnp.exp2(s - m_new[None]).reshape(tk, nq).astype(p_refs[slot].dtype)

    def weighted_values(c, slot):
        acc = acc_ref[...].reshape(acc_rows // SUBLANES, SUBLANES, nq) * a_refs[slot][...][None]
        acc_ref[...] = acc.reshape(acc_rows, nq) + jnp.dot(
            vt_ref[c], p_refs[slot][...], preferred_element_type=jnp.float32)

    def load_queries(src_ref):
        for g in range(KV_GROUP):
            qg = src_ref[:, g * HEAD_DIM:(g + 1) * HEAD_DIM].astype(jnp.float32)
            qt_ref[:, g * tq:(g + 1) * tq] = qg.T.astype(qt_ref.dtype)

    def reset_and_first_softmax():
        m_ref[...] = jnp.full(m_ref.shape, -jnp.inf, jnp.float32)
        acc_ref[...] = jnp.zeros(acc_ref.shape, jnp.float32)
        softmax(0)

    unroll = GQA_STEPS_PER_ITER
    assert unroll % 2 == 0 and (n_chunks - 2) % unroll == 0

    @pl.when(pl.program_id(1) == 0)
    def _():
        load_queries(q_ref)
        scores(0, 0)
        scores(1, 1)
        reset_and_first_softmax()

    def body(i, carry):
        for u in range(unroll):
            t = unroll * i + 1 + u
            cur = (1 + u) % 2
            scores(t + 1, 1 - cur)
            softmax(cur)
            weighted_values(t - 1, 1 - cur)
        return carry

    lax.fori_loop(0, (n_chunks - 2) // unroll, body, 0)
    load_queries(qn_ref)
    scores(0, 0)
    softmax(1)
    weighted_values(n_chunks - 2, 0)
    scores(1, 1)
    weighted_values(n_chunks - 1, 1)
    acc = acc_ref[...]
    inv_l = 1.0 / acc[HEAD_DIM:HEAD_DIM + SUBLANES, :]
    out_t = (acc[:HEAD_DIM, :].reshape(HEAD_DIM // SUBLANES, SUBLANES, nq) * inv_l[None]).reshape(HEAD_DIM, nq)
    for g in range(KV_GROUP):
        o_ref[:, g * HEAD_DIM:(g + 1) * HEAD_DIM] = out_t[:, g * tq:(g + 1) * tq].T.astype(o_ref.dtype)
    reset_and_first_softmax()


def _gqa(proj, vt, *, n_q_heads, n_kv_heads, k_col, tq=1024):
    s = proj.shape[0]
    _, n_chunks, acc_rows, tk = vt.shape
    gw = KV_GROUP * HEAD_DIM
    nq = KV_GROUP * tq
    n_blocks = s // tq
    kern = functools.partial(_gqa_kernel, tq=tq, tk=tk, n_chunks=n_chunks)
    stat = pltpu.VMEM((SUBLANES, nq), jnp.float32)
    return pl.pallas_call(
        kern,
        grid=(n_kv_heads, n_blocks),
        in_specs=[pl.BlockSpec((tq, gw), lambda h, i: (i, h)),
                  pl.BlockSpec((tq, gw), lambda h, i: (jnp.minimum(i + 1, n_blocks - 1), h)),
                  pl.BlockSpec((s, HEAD_DIM), lambda h, i: (0, k_col // HEAD_DIM + h),
                               pipeline_mode=pl.Buffered(1)),
                  pl.BlockSpec((None, n_chunks, acc_rows, tk), lambda h, i: (h, 0, 0, 0),
                               pipeline_mode=pl.Buffered(1))],
        out_specs=pl.BlockSpec((tq, gw), lambda h, i: (i, h)),
        out_shape=jax.ShapeDtypeStruct((s, n_q_heads * HEAD_DIM), jnp.bfloat16),
        scratch_shapes=[pltpu.VMEM((HEAD_DIM, nq), jnp.bfloat16),
                        pltpu.VMEM((tk, nq), jnp.float32), pltpu.VMEM((tk, nq), jnp.float32),
                        pltpu.VMEM((tk, nq), jnp.bfloat16), pltpu.VMEM((tk, nq), jnp.bfloat16),
                        stat, stat, stat, stat, stat,
                        pltpu.VMEM((acc_rows, nq), jnp.float32)],
        compiler_params=_cparams(("arbitrary", "arbitrary")),
        name="gqa",
    )(proj, proj, proj, vt)


NA_GROUP = 4
NA_WIN_ROWS = NA_GROUP + NA_ROWS


def _na_window_start(r0, n_rows):
    return jnp.clip(r0 - NA_ROWS // 2, 0, n_rows - NA_WIN_ROWS)


def _na_bias_rows(n_rows):
    rel = []
    for r0 in (0, NA_WIN_ROWS, n_rows - NA_GROUP):
        ws = int(np.clip(r0 - NA_ROWS // 2, 0, n_rows - NA_WIN_ROWS))
        per_row = []
        for i in range(NA_GROUP):
            rs = int(np.clip(r0 + i - NA_ROWS // 2, 0, n_rows - NA_ROWS))
            per_row.append([ws + wr - (r0 + i) + NA_ROWS - 1 if rs <= ws + wr < rs + NA_ROWS else None
                            for wr in range(NA_WIN_ROWS)])
        rel.append(per_row)
    return rel


def _na_build_bias(rpb_ref, b_ref, n_rows):
    shape = (GRID_W, 2 * GRID_W)
    lane = lax.broadcasted_iota(jnp.int32, shape, 1)
    c = lax.broadcasted_iota(jnp.int32, shape, 0)
    left = lane < GRID_W
    kc = jnp.where(left, lane, lane - GRID_W)
    cs = jnp.clip(c - NA_COLS // 2, 0, GRID_W - NA_COLS)
    col_ok = (kc >= cs) & (kc < cs + NA_COLS)
    neg = jnp.full(shape, NEG_BIG, jnp.float32)
    n_rel = 2 * NA_ROWS - 1
    tiles = []
    for dr in range(n_rel):
        row = jnp.broadcast_to(rpb_ref[dr:dr + 1, :], shape) * LOG2E
        lo = pltpu.roll(row, 2 * GRID_W - (NA_COLS - 1), 1, stride=1, stride_axis=0)
        hi = pltpu.roll(row, GRID_W - (NA_COLS - 1), 1, stride=1, stride_axis=0)
        tiles.append(jnp.where(col_ok, jnp.where(left, lo, hi), neg))
    rel = _na_bias_rows(n_rows)
    for v in range(3):
        for i in range(NA_GROUP):
            for j in range(NA_WIN_ROWS // 2):
                a, b = rel[v][i][2 * j], rel[v][i][2 * j + 1]
                pair = jnp.where(left, neg if a is None else tiles[a], neg if b is None else tiles[b])
                b_ref[v, i * GRID_W:(i + 1) * GRID_W, j * 2 * GRID_W:(j + 1) * 2 * GRID_W] = pair


def _natten_kernel(q_ref, k_ref, v_ref, rpb_ref, *rest, groups_per_step, n_rows, cast_periods):
    n_cast = len(cast_periods)
    cast_src, o_ref, cast_dst, b_ref = rest[:n_cast], rest[n_cast], rest[n_cast + 1:-1], rest[-1]
    gq = NA_GROUP * GRID_W
    win = NA_WIN_ROWS * GRID_W
    step = pl.program_id(0) * pl.num_programs(1) + pl.program_id(1)
    for src, dst, period in zip(cast_src, cast_dst, cast_periods):
        @pl.when(step % period == 0)
        def _():
            if len(dst.shape) == 3:
                tn = dst.shape[2]
                for n in range(dst.shape[0]):
                    dst[n] = src[:, n * tn:(n + 1) * tn].astype(dst.dtype)
            else:
                dst[...] = src[...].astype(dst.dtype)

    @pl.when(pl.program_id(1) == 0)
    def _():
        _na_build_bias(rpb_ref, b_ref, n_rows)

    for g in range(groups_per_step):
        r0 = (pl.program_id(1) * groups_per_step + g) * NA_GROUP
        variant = jnp.where(r0 == 0, 0, jnp.where(r0 == n_rows - NA_GROUP, 2, 1))
        k0 = pl.multiple_of(_na_window_start(r0, n_rows) * GRID_W, GRID_W)
        q = q_ref[g * gq:(g + 1) * gq, :]
        kw = k_ref[pl.ds(k0, win), :]
        vw = v_ref[pl.ds(k0, win), :]
        s = lax.dot_general(q, kw, _NT, preferred_element_type=jnp.float32) + b_ref[variant]
        p = jnp.exp2(s - jnp.max(s, axis=-1, keepdims=True))
        l = jnp.sum(p, axis=-1, keepdims=True)
        o = jnp.dot(p.astype(vw.dtype), vw, preferred_element_type=jnp.float32) / l
        o_ref[g * gq:(g + 1) * gq, :] = o.astype(o_ref.dtype)


def _cast_plan(w, tile_cols, n_steps, step_of):
    rows, cols = w.shape
    n_blocks = next(nb for nb in range(n_steps, 0, -1)
                    if n_steps % nb == 0 and rows % nb == 0 and (rows // nb) % (2 * SUBLANES) == 0)
    rb, period = rows // n_blocks, n_steps // n_blocks
    src = pl.BlockSpec((rb, cols), lambda h, i: (step_of(h, i) // period, 0))
    if tile_cols is None:
        dst = pl.BlockSpec((rb, cols), lambda h, i: (step_of(h, i) // period, 0))
        shape = jax.ShapeDtypeStruct((rows, cols), jnp.bfloat16)
    else:
        n_tiles = cols // tile_cols
        dst = pl.BlockSpec((n_tiles, rb, tile_cols), lambda h, i: (0, step_of(h, i) // period, 0))
        shape = jax.ShapeDtypeStruct((n_tiles, rows, tile_cols), jnp.bfloat16)
    return src, dst, shape, period


def _natten(proj, rpb, casts, *, n_heads, q_col, k_col, v_col, groups_per_step=16):
    s = proj.shape[0]
    n_rows = s // GRID_W
    tq = groups_per_step * NA_GROUP * GRID_W
    gq, win = NA_GROUP * GRID_W, NA_WIN_ROWS * GRID_W
    assert n_rows % (groups_per_step * NA_GROUP) == 0 and n_rows >= 2 * NA_WIN_ROWS
    assert NA_WIN_ROWS % 2 == 0 and 2 * GRID_W == HEAD_DIM and rpb.shape[2] <= GRID_W
    rel_rows = -(-rpb.shape[1] // SUBLANES) * SUBLANES
    rpb_rows = jnp.pad(rpb, ((0, 0), (0, rel_rows - rpb.shape[1]), (0, 2 * GRID_W - rpb.shape[2])))
    n_i = s // tq
    plans = [_cast_plan(w, tile_cols, n_heads * n_i, lambda h, i: h * n_i + i) for w, tile_cols in casts]
    kern = functools.partial(_natten_kernel, groups_per_step=groups_per_step, n_rows=n_rows,
                             cast_periods=tuple(p[3] for p in plans))
    out = pl.pallas_call(
        kern,
        grid=(n_heads, n_i),
        in_specs=[pl.BlockSpec((tq, HEAD_DIM), lambda h, i: (i, q_col // HEAD_DIM + h)),
                  pl.BlockSpec((s, HEAD_DIM), lambda h, i: (0, k_col // HEAD_DIM + h)),
                  pl.BlockSpec((s, HEAD_DIM), lambda h, i: (0, v_col // HEAD_DIM + h)),
                  pl.BlockSpec((None, rel_rows, 2 * GRID_W), lambda h, i: (h, 0, 0))] + [p[0] for p in plans],
        out_specs=[pl.BlockSpec((tq, HEAD_DIM), lambda h, i: (i, h))] + [p[1] for p in plans],
        out_shape=[jax.ShapeDtypeStruct((s, n_heads * HEAD_DIM), jnp.bfloat16)] + [p[2] for p in plans],
        scratch_shapes=[pltpu.VMEM((3, gq, win), jnp.float32)],
        compiler_params=_cparams(("arbitrary", "arbitrary")),
        name="natten",
    )(proj, proj, proj, rpb_rows, *[w for w, _ in casts])
    return out[0], out[1:]


def _merge_kernel(ya_ref, yb_ref, ga_ref, gb_ref, woa_ref, wob_ref, wout_ref, x_ref, g1_ref, n2_ref, sc_ref,
                  sh_ref, x1_ref, h2_ref, m_ref, *, tn, n_tiles):
    n = pl.program_id(1)
    a = jnp.dot(ya_ref[...], woa_ref[n], preferred_element_type=jnp.float32)
    b = jnp.dot(yb_ref[...], wob_ref[n], preferred_element_type=jnp.float32)
    m = (jax.nn.sigmoid(ga_ref[...].astype(jnp.float32)) * a
         + jax.nn.sigmoid(gb_ref[...].astype(jnp.float32)) * b)
    col = pl.multiple_of(n * tn, tn)
    m_ref[:, pl.ds(col, tn)] = m.astype(m_ref.dtype)

    @pl.when(n == n_tiles - 1)
    def _():
        y = jnp.dot(m_ref[...], wout_ref[...], preferred_element_type=jnp.float32)
        x1 = x_ref[...] + g1_ref[...] * y
        x1_ref[...] = x1
        h2_ref[...] = _rms_modulate(x1, n2_ref[...], sc_ref[...], sh_ref[...]).astype(h2_ref.dtype)


MERGE_TILE = 512


def _merge(ya, yb, proj, woa_tiles, wob_tiles, w_out, x, g1, n2, sc2, sh2, *, ga_col, gb_col, tm=512):
    s, d = x.shape
    da = ya.shape[1]
    n_tiles, _, tn = woa_tiles.shape
    kern = functools.partial(_merge_kernel, tn=tn, n_tiles=n_tiles)
    row = lambda i, n: (0, 0)
    tiles = pl.BlockSpec((n_tiles, da, tn), lambda i, n: (0, 0, 0), pipeline_mode=pl.Buffered(1))
    return pl.pallas_call(
        kern,
        grid=(s // tm, n_tiles),
        in_specs=[pl.BlockSpec((tm, da), lambda i, n: (i, 0)),
                  pl.BlockSpec((tm, da), lambda i, n: (i, 0)),
                  pl.BlockSpec((tm, tn), lambda i, n: (i, ga_col // tn + n)),
                  pl.BlockSpec((tm, tn), lambda i, n: (i, gb_col // tn + n)),
                  tiles, tiles,
                  pl.BlockSpec((d, d), row, pipeline_mode=pl.Buffered(1)),
                  pl.BlockSpec((tm, d), lambda i, n: (i, 0)),
                  pl.BlockSpec((1, d), row), pl.BlockSpec((1, d), row),
                  pl.BlockSpec((1, d), row), pl.BlockSpec((1, d), row)],
        out_specs=[pl.BlockSpec((tm, d), lambda i, n: (i, 0)),
                   pl.BlockSpec((tm, d), lambda i, n: (i, 0))],
        out_shape=[jax.ShapeDtypeStruct((s, d), jnp.float32),
                   jax.ShapeDtypeStruct((s, d), jnp.bfloat16)],
        scratch_shapes=[pltpu.VMEM((tm, d), jnp.bfloat16)],
        compiler_params=_cparams(("parallel", "arbitrary")),
        name="merge",
    )(ya, yb, proj, proj, woa_tiles, wob_tiles, w_out, x, g1, n2, sc2, sh2)


def _ffn_kernel(h_ref, wg_ref, wu_ref, wd_ref, x1_ref, g2_ref, fw_ref, o_ref, acc_ref, *, n_tiles, final_norm):
    f = pl.program_id(1)

    @pl.when(f == 0)
    def _():
        acc_ref[...] = jnp.zeros(acc_ref.shape, jnp.float32)

    h = h_ref[...]
    g = jnp.dot(h, wg_ref[...], preferred_element_type=jnp.float32)
    u = jnp.dot(h, wu_ref[...], preferred_element_type=jnp.float32)
    a = (g * jax.nn.sigmoid(g) * u).astype(h.dtype)
    acc_ref[...] += jnp.dot(a, wd_ref[...], preferred_element_type=jnp.float32)

    @pl.when(f == n_tiles - 1)
    def _():
        x2 = x1_ref[...] + g2_ref[...] * acc_ref[...]
        if final_norm:
            x2 = x2 * lax.rsqrt(jnp.mean(x2 * x2, axis=-1, keepdims=True) + NORM_EPS) * fw_ref[...]
        o_ref[...] = x2


def _ffn(h2, w_gate, w_up, w_down, x1, g2, final_w, *, final_norm, tm=512, tf=512):
    s, d = x1.shape
    dff = w_gate.shape[1]
    n_tiles = dff // tf
    kern = functools.partial(_ffn_kernel, n_tiles=n_tiles, final_norm=final_norm)
    row = lambda i, f: (0, 0)
    return pl.pallas_call(
        kern,
        grid=(s // tm, n_tiles),
        in_specs=[pl.BlockSpec((tm, d), lambda i, f: (i, 0)),
                  pl.BlockSpec((d, tf), lambda i, f: (0, f)),
                  pl.BlockSpec((d, tf), lambda i, f: (0, f)),
                  pl.BlockSpec((tf, d), lambda i, f: (f, 0)),
                  pl.BlockSpec((tm, d), lambda i, f: (i, 0)),
                  pl.BlockSpec((1, d), row), pl.BlockSpec((1, d), row)],
        out_specs=pl.BlockSpec((tm, d), lambda i, f: (i, 0)),
        out_shape=jax.ShapeDtypeStruct((s, d), jnp.float32),
        scratch_shapes=[pltpu.VMEM((tm, d), jnp.float32)],
        compiler_params=_cparams(("parallel", "arbitrary")),
        name="ffn",
    )(h2, w_gate, w_up, w_down, x1, g2, final_w)


def _rope_tables(s):
    n_rows = s // GRID_W
    axis_dim = HEAD_DIM // 2
    inv = ROPE_THETA ** (-jnp.arange(0, axis_dim, 2, dtype=jnp.float32) / axis_dim)
    lane = np.arange(HEAD_DIM)
    inv_lane = inv[(lane // 2) % (axis_dim // 2)]
    by_col = (lane >= axis_dim)[None, :]
    even = (lane % 2 == 0)[None, :]

    def parts(n, mine):
        ang = jnp.arange(n, dtype=jnp.float32)[:, None] * inv_lane[None]
        cos, sin = jnp.cos(ang), jnp.sin(ang)
        tabs = jnp.stack([cos, jnp.where(even, -sin, 0.0), jnp.where(even, 0.0, sin)])
        return jnp.where(mine[None], tabs, 0.0)

    return parts(n_rows, ~by_col), parts(GRID_W, by_col)


def kernel(x, c, w_ada, b_ada, norm1_w, w_in, q_norm_w, k_norm_w, nat_rpb, w_oa, w_ob, w_out, norm2_w,
           w_ffn_gate, w_ffn_up, w_ffn_down, final_w):
    b, s, d = x.shape
    assert b == 1
    depth = w_ada.shape[0]
    n_heads_a = w_oa.shape[1] // HEAD_DIM
    n_heads_b = w_ob.shape[1] // HEAD_DIM
    n_kv_a = n_heads_a // KV_GROUP
    splits = (n_heads_a * HEAD_DIM, n_kv_a * HEAD_DIM, n_kv_a * HEAD_DIM,
              n_heads_b * HEAD_DIM, n_heads_b * HEAD_DIM, n_heads_b * HEAD_DIM, d, d)
    offs = [0] + np.cumsum(splits)[:-1].tolist()
    qa_col, ka_col, va_col, qb_col, kb_col, vb_col, ga_col, gb_col = offs
    assert qa_col == 0

    bf = jnp.bfloat16
    assert va_col == ka_col + n_kv_a * HEAD_DIM
    rope_rows, rope_cols = _rope_tables(s)
    xs = x[0]
    c_col = c.reshape(d, 1)
    for l in range(depth):
        mod = _adaln(c_col, w_ada[l], b_ada[l][None, :])
        sh1, sc1, g1, sh2, sc2, g2 = [mod[:, i * d:(i + 1) * d] for i in range(6)]
        proj, vt = _inproj(xs, norm1_w[l][None, :], sc1, sh1, w_in[l].astype(bf),
                           q_norm_w[l][None, :], k_norm_w[l][None, :], rope_rows, rope_cols,
                           n_q_heads=n_heads_a, n_k_heads=n_kv_a, qb_cols=(qb_col, kb_col),
                           q_scale=HEAD_DIM ** -0.5 * LOG2E)
        ya = _gqa(proj, vt, n_q_heads=n_heads_a, n_kv_heads=n_kv_a, k_col=ka_col)
        yb, (w_gate, w_up, w_down, w_o, woa_tiles, wob_tiles) = _natten(
            proj, nat_rpb[l],
            [(w_ffn_gate[l], None), (w_ffn_up[l], None), (w_ffn_down[l], None), (w_out[l], None),
             (w_oa[l], MERGE_TILE), (w_ob[l], MERGE_TILE)],
            n_heads=n_heads_b, q_col=qb_col, k_col=kb_col, v_col=vb_col)
        xs, h2 = _merge(ya, yb, proj, woa_tiles, wob_tiles, w_o, xs, g1, norm2_w[l][None, :], sc2, sh2,
                        ga_col=ga_col, gb_col=gb_col)
        xs = _ffn(h2, w_gate, w_up, w_down, xs, g2, final_w[None, :], final_norm=(l == depth - 1))
    return xs[None]
```

```python
import functools
import math

import jax
import jax.numpy as jnp
import numpy as np
from jax import lax
from jax.experimental import pallas as pl
from jax.experimental.pallas import tpu as pltpu

HEAD_DIM = 128
KV_GROUP = 4
GRID_W = 64
NA_ROWS = 8
NA_COLS = 16
ROPE_THETA = 10000.0
NORM_EPS = 1e-6
NEG_BIG = -1e30
LOG2E = 1.4426950408889634
SUBLANES = 8
GQA_KEY_CHUNK = 512
GQA_ONES_ROWS = 16
GQA_STEPS_PER_ITER = 2

VMEM_LIMIT = 56 * 1024 * 1024

_NT = (((1,), (1,)), ((), ()))


def _cparams(sem):
    return pltpu.CompilerParams(dimension_semantics=sem, vmem_limit_bytes=VMEM_LIMIT)


def _adaln_kernel(c_ref, w_ref, b_ref, o_ref):
    c = c_ref[...]
    act = c * jax.nn.sigmoid(c)
    o_ref[...] = jnp.sum(act * w_ref[...], axis=0, keepdims=True) + b_ref[...]


def _adaln(c_col, w_ada, b_ada, tn=1024):
    d, n = w_ada.shape
    return pl.pallas_call(
        _adaln_kernel,
        grid=(n // tn,),
        in_specs=[pl.BlockSpec((d, 1), lambda j: (0, 0)),
                  pl.BlockSpec((d, tn), lambda j: (0, j)),
                  pl.BlockSpec((1, tn), lambda j: (0, j))],
        out_specs=pl.BlockSpec((1, tn), lambda j: (0, j)),
        out_shape=jax.ShapeDtypeStruct((1, n), jnp.float32),
        compiler_params=_cparams(("arbitrary",)),
        name="adaln",
    )(c_col, w_ada, b_ada)


def _rms_modulate(x, w, sc, sh):
    rstd = lax.rsqrt(jnp.mean(x * x, axis=-1, keepdims=True) + NORM_EPS)
    return (x * rstd) * (w * (1.0 + sc)) + sh


def _head_norm_rope(t, w, cos, sin_e, sin_o, scale):
    t = t * lax.rsqrt(jnp.mean(t * t, axis=-1, keepdims=True) + NORM_EPS) * w
    nxt = pltpu.roll(t, HEAD_DIM - 1, 1)
    prv = pltpu.roll(t, 1, 1)
    return (t * cos + nxt * sin_e + prv * sin_o) * scale


def _inproj_kernel(x_ref, n1_ref, sc_ref, sh_ref, w_ref, qw_ref, kw_ref, rt_ref, ct_ref,
                   o_ref, vt_ref, h_ref, rope_ref, *, n_q_tiles, heads_per_tile, n_k_heads, qb_tiles, q_scale):
    j = pl.program_id(1)
    tm = x_ref.shape[0]

    @pl.when(j == 0)
    def _():
        h_ref[...] = _rms_modulate(x_ref[...], n1_ref[...], sc_ref[...], sh_ref[...]).astype(h_ref.dtype)
        for k in range(3):
            for r in range(tm // GRID_W):
                rope_ref[k, r * GRID_W:(r + 1) * GRID_W, :] = rt_ref[k, r:r + 1, :] + ct_ref[k]

    acc = jnp.dot(h_ref[...], w_ref[...].astype(h_ref.dtype), preferred_element_type=jnp.float32)

    def rope_heads(n_heads, w, scale):
        cos, se, so = rope_ref[0], rope_ref[1], rope_ref[2]
        for hh in range(heads_per_tile):
            sl = slice(hh * HEAD_DIM, (hh + 1) * HEAD_DIM)
            t = acc[:, sl]
            if hh < n_heads:
                t = _head_norm_rope(t, w, cos, se, so, scale)
            o_ref[:, sl] = t.astype(o_ref.dtype)

    @pl.when(j < n_q_tiles)
    def _():
        rope_heads(heads_per_tile, qw_ref[...], q_scale)

    @pl.when(j == n_q_tiles)
    def _():
        rope_heads(n_k_heads, kw_ref[...], 1.0)
        n_v, n_chunks, _, tk = vt_ref.shape
        for vh in range(n_v):
            col = (n_k_heads + vh) * HEAD_DIM
            for cc in range(n_chunks):
                v = acc[cc * tk:(cc + 1) * tk, col:col + HEAD_DIM]
                vt_ref[vh, cc, :HEAD_DIM, :] = v.T.astype(vt_ref.dtype)
                vt_ref[vh, cc, HEAD_DIM:, :] = jnp.ones((GQA_ONES_ROWS, tk), vt_ref.dtype)

    in_qb = (j >= qb_tiles[0]) & (j < qb_tiles[1])

    @pl.when(in_qb)
    def _():
        o_ref[...] = (acc * q_scale).astype(o_ref.dtype)

    @pl.when((j > n_q_tiles) & jnp.logical_not(in_qb))
    def _():
        o_ref[...] = acc.astype(o_ref.dtype)


def _inproj(x, n1, sc1, sh1, w_in, qw, kw, rope_rows, rope_cols, *, n_q_heads, n_k_heads, qb_cols, q_scale,
            tm=1024, tn=512, tk=GQA_KEY_CHUNK):
    s, d = x.shape
    n = w_in.shape[1]
    heads_per_tile = tn // HEAD_DIM
    assert n_q_heads % heads_per_tile == 0 and 2 * n_k_heads == heads_per_tile
    assert qb_cols[0] % tn == 0 and qb_cols[1] % tn == 0 and qb_cols[0] // tn > n_q_heads // heads_per_tile
    assert tm % tk == 0 and tm % GRID_W == 0 and (tm // GRID_W) % SUBLANES == 0
    kern = functools.partial(_inproj_kernel, n_q_tiles=n_q_heads // heads_per_tile,
                             heads_per_tile=heads_per_tile, n_k_heads=n_k_heads,
                             qb_tiles=(qb_cols[0] // tn, qb_cols[1] // tn), q_scale=q_scale)
    row = lambda i, j: (0, 0)
    vt_rows = HEAD_DIM + GQA_ONES_ROWS
    return pl.pallas_call(
        kern,
        grid=(s // tm, n // tn),
        in_specs=[pl.BlockSpec((tm, d), lambda i, j: (i, 0)),
                  pl.BlockSpec((1, d), row), pl.BlockSpec((1, d), row), pl.BlockSpec((1, d), row),
                  pl.BlockSpec((d, tn), lambda i, j: (0, j)),
                  pl.BlockSpec((1, HEAD_DIM), row), pl.BlockSpec((1, HEAD_DIM), row),
                  pl.BlockSpec((3, tm // GRID_W, HEAD_DIM), lambda i, j: (0, i, 0)),
                  pl.BlockSpec((3, GRID_W, HEAD_DIM), lambda i, j: (0, 0, 0))],
        out_specs=[pl.BlockSpec((tm, tn), lambda i, j: (i, j)),
                   pl.BlockSpec((n_k_heads, tm // tk, vt_rows, tk), lambda i, j: (0, i, 0, 0))],
        out_shape=[jax.ShapeDtypeStruct((s, n), jnp.bfloat16),
                   jax.ShapeDtypeStruct((n_k_heads, s // tk, vt_rows, tk), jnp.bfloat16)],
        scratch_shapes=[pltpu.VMEM((tm, d), jnp.bfloat16),
                        pltpu.VMEM((3, tm, HEAD_DIM), jnp.float32)],
        compiler_params=_cparams(("parallel", "arbitrary")),
        name="inproj",
    )(x, n1, sc1, sh1, w_in, qw, kw, rope_rows, rope_cols)


def _sublane_allreduce(x, op):
    shift = SUBLANES // 2
    while shift:
        x = op(x, pltpu.roll(x, shift, 0))
        shift //= 2
    return x


def _gqa_kernel(q_ref, qn_ref, k_ref, vt_ref, o_ref, qt_ref, s0_ref, s1_ref, p0_ref, p1_ref, x0_ref, x1_ref,
                a0_ref, a1_ref, m_ref, acc_ref, *, tq, tk, n_chunks):
    s_refs, p_refs, x_refs, a_refs = (s0_ref, s1_ref), (p0_ref, p1_ref), (x0_ref, x1_ref), (a0_ref, a1_ref)
    nq = KV_GROUP * tq
    acc_rows = HEAD_DIM + GQA_ONES_ROWS

    def scores(c, slot):
        k = k_ref[pl.ds(pl.multiple_of(c * tk, tk), tk), :]
        s = jnp.dot(k, qt_ref[...], preferred_element_type=jnp.float32)
        s_refs[slot][...] = s
        x_refs[slot][...] = jnp.max(s.reshape(tk // SUBLANES, SUBLANES, nq), axis=0)

    def softmax(slot):
        m_old = m_ref[...]
        m_new = jnp.maximum(m_old, _sublane_allreduce(x_refs[slot][...], jnp.maximum))
        a_refs[slot][...] = jnp.exp2(m_old - m_new)
        m_ref[...] = m_new
        s = s_refs[slot][...].reshape(tk // SUBLANES, SUBLANES, nq)
        p_refs[slot][...] = jnp.exp2(s - m_new[None]).reshape(tk, nq).astype(p_refs[slot].dtype)

    def weighted_values(c, slot):
        acc = acc_ref[...].reshape(acc_rows // SUBLANES, SUBLANES, nq) * a_refs[slot][...][None]
        acc_ref[...] = acc.reshape(acc_rows, nq) + jnp.dot(
            vt_ref[c], p_refs[slot][...], preferred_element_type=jnp.float32)

    def load_queries(src_ref):
        for g in range(KV_GROUP):
            qg = src_ref[:, g * HEAD_DIM:(g + 1) * HEAD_DIM].astype(jnp.float32)
            qt_ref[:, g * tq:(g + 1) * tq] = qg.T.astype(qt_ref.dtype)

    def reset_and_first_softmax():
        m_ref[...] = jnp.full(m_ref.shape, -jnp.inf, jnp.float32)
        acc_ref[...] = jnp.zeros(acc_ref.shape, jnp.float32)
        softmax(0)

    unroll = GQA_STEPS_PER_ITER
    assert unroll % 2 == 0 and (n_chunks - 2) % unroll == 0

    @pl.when(pl.program_id(1) == 0)
    def _():
        load_queries(q_ref)
        scores(0, 0)
        scores(1, 1)
        reset_and_first_softmax()

    def body(i, carry):
        for u in range(unroll):
            t = unroll * i + 1 + u
            cur = (1 + u) % 2
            scores(t + 1, 1 - cur)
            softmax(cur)
            weighted_values(t - 1, 1 - cur)
        return carry

    lax.fori_loop(0, (n_chunks - 2) // unroll, body, 0)
    load_queries(qn_ref)
    scores(0, 0)
    softmax(1)
    weighted_values(n_chunks - 2, 0)
    scores(1, 1)
    weighted_values(n_chunks - 1, 1)
    acc = acc_ref[...]
    inv_l = 1.0 / acc[HEAD_DIM:HEAD_DIM + SUBLANES, :]
    out_t = (acc[:HEAD_DIM, :].reshape(HEAD_DIM // SUBLANES, SUBLANES, nq) * inv_l[None]).reshape(HEAD_DIM, nq)
    for g in range(KV_GROUP):
        o_ref[:, g * HEAD_DIM:(g + 1) * HEAD_DIM] = out_t[:, g * tq:(g + 1) * tq].T.astype(o_ref.dtype)
    reset_and_first_softmax()


def _gqa(proj, vt, *, n_q_heads, n_kv_heads, k_col, tq=1024):
    s = proj.shape[0]
    _, n_chunks, acc_rows, tk = vt.shape
    gw = KV_GROUP * HEAD_DIM
    nq = KV_GROUP * tq
    n_blocks = s // tq
    kern = functools.partial(_gqa_kernel, tq=tq, tk=tk, n_chunks=n_chunks)
    stat = pltpu.VMEM((SUBLANES, nq), jnp.float32)
    return pl.pallas_call(
        kern,
        grid=(n_kv_heads, n_blocks),
        in_specs=[pl.BlockSpec((tq, gw), lambda h, i: (i, h)),
                  pl.BlockSpec((tq, gw), lambda h, i: (jnp.minimum(i + 1, n_blocks - 1), h)),
                  pl.BlockSpec((s, HEAD_DIM), lambda h, i: (0, k_col // HEAD_DIM + h),
                               pipeline_mode=pl.Buffered(1)),
                  pl.BlockSpec((None, n_chunks, acc_rows, tk), lambda h, i: (h, 0, 0, 0),
                               pipeline_mode=pl.Buffered(1))],
        out_specs=pl.BlockSpec((tq, gw), lambda h, i: (i, h)),
        out_shape=jax.ShapeDtypeStruct((s, n_q_heads * HEAD_DIM), jnp.bfloat16),
        scratch_shapes=[pltpu.VMEM((HEAD_DIM, nq), jnp.bfloat16),
                        pltpu.VMEM((tk, nq), jnp.float32), pltpu.VMEM((tk, nq), jnp.float32),
                        pltpu.VMEM((tk, nq), jnp.bfloat16), pltpu.VMEM((tk, nq), jnp.bfloat16),
                        stat, stat, stat, stat, stat,
                        pltpu.VMEM((acc_rows, nq), jnp.float32)],
        compiler_params=_cparams(("arbitrary", "arbitrary")),
        name="gqa",
    )(proj, proj, proj, vt)


NA_GROUP = 4
NA_WIN_ROWS = NA_GROUP + NA_ROWS


def _na_window_start(r0, n_rows):
    return jnp.clip(r0 - NA_ROWS // 2, 0, n_rows - NA_WIN_ROWS)


def _na_bias_rows(n_rows):
    rel = []
    for r0 in (0, NA_WIN_ROWS, n_rows - NA_GROUP):
        ws = int(np.clip(r0 - NA_ROWS // 2, 0, n_rows - NA_WIN_ROWS))
        per_row = []
        for i in range(NA_GROUP):
            rs = int(np.clip(r0 + i - NA_ROWS // 2, 0, n_rows - NA_ROWS))
            per_row.append([ws + wr - (r0 + i) + NA_ROWS - 1 if rs <= ws + wr < rs + NA_ROWS else None
                            for wr in range(NA_WIN_ROWS)])
        rel.append(per_row)
    return rel


def _na_build_bias(rpb_ref, b_ref, n_rows):
    shape = (GRID_W, 2 * GRID_W)
    lane = lax.broadcasted_iota(jnp.int32, shape, 1)
    c = lax.broadcasted_iota(jnp.int32, shape, 0)
    left = lane < GRID_W
    kc = jnp.where(left, lane, lane - GRID_W)
    cs = jnp.clip(c - NA_COLS // 2, 0, GRID_W - NA_COLS)
    col_ok = (kc >= cs) & (kc < cs + NA_COLS)
    neg = jnp.full(shape, NEG_BIG, jnp.float32)
    n_rel = 2 * NA_ROWS - 1
    tiles = []
    for dr in range(n_rel):
        row = jnp.broadcast_to(rpb_ref[dr:dr + 1, :], shape) * LOG2E
        lo = pltpu.roll(row, 2 * GRID_W - (NA_COLS - 1), 1, stride=1, stride_axis=0)
        hi = pltpu.roll(row, GRID_W - (NA_COLS - 1), 1, stride=1, stride_axis=0)
        tiles.append(jnp.where(col_ok, jnp.where(left, lo, hi), neg))
    rel = _na_bias_rows(n_rows)
    for v in range(3):
        for i in range(NA_GROUP):
            for j in range(NA_WIN_ROWS // 2):
                a, b = rel[v][i][2 * j], rel[v][i][2 * j + 1]
                pair = jnp.where(left, neg if a is None else tiles[a], neg if b is None else tiles[b])
                b_ref[v, i * GRID_W:(i + 1) * GRID_W, j * 2 * GRID_W:(j + 1) * 2 * GRID_W] = pair


def _natten_kernel(q_ref, k_ref, v_ref, rpb_ref, *rest, groups_per_step, n_rows, cast_periods):
    n_cast = len(cast_periods)
    cast_src, o_ref, cast_dst, b_ref = rest[:n_cast], rest[n_cast], rest[n_cast + 1:-1], rest[-1]
    gq = NA_GROUP * GRID_W
    win = NA_WIN_ROWS * GRID_W
    step = pl.program_id(0) * pl.num_programs(1) + pl.program_id(1)
    for src, dst, period in zip(cast_src, cast_dst, cast_periods):
        @pl.when(step % period == 0)
        def _():
            if len(dst.shape) == 3:
                tn = dst.shape[2]
                for n in range(dst.shape[0]):
                    dst[n] = src[:, n * tn:(n + 1) * tn].astype(dst.dtype)
            else:
                dst[...] = src[...].astype(dst.dtype)

    @pl.when(pl.program_id(1) == 0)
    def _():
        _na_build_bias(rpb_ref, b_ref, n_rows)

    for g in range(groups_per_step):
        r0 = (pl.program_id(1) * groups_per_step + g) * NA_GROUP
        variant = jnp.where(r0 == 0, 0, jnp.where(r0 == n_rows - NA_GROUP, 2, 1))
        k0 = pl.multiple_of(_na_window_start(r0, n_rows) * GRID_W, GRID_W)
        q = q_ref[g * gq:(g + 1) * gq, :]
        kw = k_ref[pl.ds(k0, win), :]
        vw = v_ref[pl.ds(k0, win), :]
        s = lax.dot_general(q, kw, _NT, preferred_element_type=jnp.float32) + b_ref[variant]
        p = jnp.exp2(s - jnp.max(s, axis=-1, keepdims=True))
        l = jnp.sum(p, axis=-1, keepdims=True)
        o = jnp.dot(p.astype(vw.dtype), vw, preferred_element_type=jnp.float32) / l
        o_ref[g * gq:(g + 1) * gq, :] = o.astype(o_ref.dtype)


def _cast_plan(w, tile_cols, n_steps, step_of):
    rows, cols = w.shape
    n_blocks = next(nb for nb in range(n_steps, 0, -1)
                    if n_steps % nb == 0 and rows % nb == 0 and (rows // nb) % (2 * SUBLANES) == 0)
    rb, period = rows // n_blocks, n_steps // n_blocks
    src = pl.BlockSpec((rb, cols), lambda h, i: (step_of(h, i) // period, 0))
    if tile_cols is None:
        dst = pl.BlockSpec((rb, cols), lambda h, i: (step_of(h, i) // period, 0))
        shape = jax.ShapeDtypeStruct((rows, cols), jnp.bfloat16)
    else:
        n_tiles = cols // tile_cols
        dst = pl.BlockSpec((n_tiles, rb, tile_cols), lambda h, i: (0, step_of(h, i) // period, 0))
        shape = jax.ShapeDtypeStruct((n_tiles, rows, tile_cols), jnp.bfloat16)
    return src, dst, shape, period


def _natten(proj, rpb, casts, *, n_heads, q_col, k_col, v_col, groups_per_step=16):
    s = proj.shape[0]
    n_rows = s // GRID_W
    tq = groups_per_step * NA_GROUP * GRID_W
    gq, win = NA_GROUP * GRID_W, NA_WIN_ROWS * GRID_W
    assert n_rows % (groups_per_step * NA_GROUP) == 0 and n_rows >= 2 * NA_WIN_ROWS
    assert NA_WIN_ROWS % 2 == 0 and 2 * GRID_W == HEAD_DIM and rpb.shape[2] <= GRID_W
    rel_rows = -(-rpb.shape[1] // SUBLANES) * SUBLANES
    rpb_rows = jnp.pad(rpb, ((0, 0), (0, rel_rows - rpb.shape[1]), (0, 2 * GRID_W - rpb.shape[2])))
    n_i = s // tq
    plans = [_cast_plan(w, tile_cols, n_heads * n_i, lambda h, i: h * n_i + i) for w, tile_cols in casts]
    kern = functools.partial(_natten_kernel, groups_per_step=groups_per_step, n_rows=n_rows,
                             cast_periods=tuple(p[3] for p in plans))
    out = pl.pallas_call(
        kern,
        grid=(n_heads, n_i),
        in_specs=[pl.BlockSpec((tq, HEAD_DIM), lambda h, i: (i, q_col // HEAD_DIM + h)),
                  pl.BlockSpec((s, HEAD_DIM), lambda h, i: (0, k_col // HEAD_DIM + h)),
                  pl.BlockSpec((s, HEAD_DIM), lambda h, i: (0, v_col // HEAD_DIM + h)),
                  pl.BlockSpec((None, rel_rows, 2 * GRID_W), lambda h, i: (h, 0, 0))] + [p[0] for p in plans],
        out_specs=[pl.BlockSpec((tq, HEAD_DIM), lambda h, i: (i, h))] + [p[1] for p in plans],
        out_shape=[jax.ShapeDtypeStruct((s, n_heads * HEAD_DIM), jnp.bfloat16)] + [p[2] for p in plans],
        scratch_shapes=[pltpu.VMEM((3, gq, win), jnp.float32)],
        compiler_params=_cparams(("arbitrary", "arbitrary")),
        name="natten",
    )(proj, proj, proj, rpb_rows, *[w for w, _ in casts])
    return out[0], out[1:]


def _merge_kernel(ya_ref, yb_ref, ga_ref, gb_ref, woa_ref, wob_ref, wout_ref, x_ref, g1_ref, n2_ref, sc_ref,
                  sh_ref, x1_ref, h2_ref, m_ref, *, tn, n_tiles):
    n = pl.program_id(1)
    a = jnp.dot(ya_ref[...], woa_ref[n], preferred_element_type=jnp.float32)
    b = jnp.dot(yb_ref[...], wob_ref[n], preferred_element_type=jnp.float32)
    m = (jax.nn.sigmoid(ga_ref[...].astype(jnp.float32)) * a
         + jax.nn.sigmoid(gb_ref[...].astype(jnp.float32)) * b)
    col = pl.multiple_of(n * tn, tn)
    m_ref[:, pl.ds(col, tn)] = m.astype(m_ref.dtype)

    @pl.when(n == n_tiles - 1)
    def _():
        y = jnp.dot(m_ref[...], wout_ref[...], preferred_element_type=jnp.float32)
        x1 = x_ref[...] + g1_ref[...] * y
        x1_ref[...] = x1
        h2_ref[...] = _rms_modulate(x1, n2_ref[...], sc_ref[...], sh_ref[...]).astype(h2_ref.dtype)


MERGE_TILE = 512


def _merge(ya, yb, proj, woa_tiles, wob_tiles, w_out, x, g1, n2, sc2, sh2, *, ga_col, gb_col, tm=512):
    s, d = x.shape
    da = ya.shape[1]
    n_tiles, _, tn = woa_tiles.shape
    kern = functools.partial(_merge_kernel, tn=tn, n_tiles=n_tiles)
    row = lambda i, n: (0, 0)
    tiles = pl.BlockSpec((n_tiles, da, tn), lambda i, n: (0, 0, 0), pipeline_mode=pl.Buffered(1))
    return pl.pallas_call(
        kern,
        grid=(s // tm, n_tiles),
        in_specs=[pl.BlockSpec((tm, da), lambda i, n: (i, 0)),
                  pl.BlockSpec((tm, da), lambda i, n: (i, 0)),
                  pl.BlockSpec((tm, tn), lambda i, n: (i, ga_col // tn + n)),
                  pl.BlockSpec((tm, tn), lambda i, n: (i, gb_col // tn + n)),
                  tiles, tiles,
                  pl.BlockSpec((d, d), row, pipeline_mode=pl.Buffered(1)),
                  pl.BlockSpec((tm, d), lambda i, n: (i, 0)),
                  pl.BlockSpec((1, d), row), pl.BlockSpec((1, d), row),
                  pl.BlockSpec((1, d), row), pl.BlockSpec((1, d), row)],
        out_specs=[pl.BlockSpec((tm, d), lambda i, n: (i, 0)),
                   pl.BlockSpec((tm, d), lambda i, n: (i, 0))],
        out_shape=[jax.ShapeDtypeStruct((s, d), jnp.float32),
                   jax.ShapeDtypeStruct((s, d), jnp.bfloat16)],
        scratch_shapes=[pltpu.VMEM((tm, d), jnp.bfloat16)],
        compiler_params=_cparams(("parallel", "arbitrary")),
        name="merge",
    )(ya, yb, proj, proj, woa_tiles, wob_tiles, w_out, x, g1, n2, sc2, sh2)


def _ffn_kernel(h_ref, wg_ref, wu_ref, wd_ref, x1_ref, g2_ref, fw_ref, o_ref, acc_ref, *, n_tiles, final_norm):
    f = pl.program_id(1)

    @pl.when(f == 0)
    def _():
        acc_ref[...] = jnp.zeros(acc_ref.shape, jnp.float32)

    h = h_ref[...]
    g = jnp.dot(h, wg_ref[...], preferred_element_type=jnp.float32)
    u = jnp.dot(h, wu_ref[...], preferred_element_type=jnp.float32)
    a = (g * jax.nn.sigmoid(g) * u).astype(h.dtype)
    acc_ref[...] += jnp.dot(a, wd_ref[...], preferred_element_type=jnp.float32)

    @pl.when(f == n_tiles - 1)
    def _():
        x2 = x1_ref[...] + g2_ref[...] * acc_ref[...]
        if final_norm:
            x2 = x2 * lax.rsqrt(jnp.mean(x2 * x2, axis=-1, keepdims=True) + NORM_EPS) * fw_ref[...]
        o_ref[...] = x2


def _ffn(h2, w_gate, w_up, w_down, x1, g2, final_w, *, final_norm, tm=512, tf=512):
    s, d = x1.shape
    dff = w_gate.shape[1]
    n_tiles = dff // tf
    kern = functools.partial(_ffn_kernel, n_tiles=n_tiles, final_norm=final_norm)
    row = lambda i, f: (0, 0)
    return pl.pallas_call(
        kern,
        grid=(s // tm, n_tiles),
        in_specs=[pl.BlockSpec((tm, d), lambda i, f: (i, 0)),
                  pl.BlockSpec((d, tf), lambda i, f: (0, f)),
                  pl.BlockSpec((d, tf), lambda i, f: (0, f)),
                  pl.BlockSpec((tf, d), lambda i, f: (f, 0)),
                  pl.BlockSpec((tm, d), lambda i, f: (i, 0)),
                  pl.BlockSpec((1, d), row), pl.BlockSpec((1, d), row)],
        out_specs=pl.BlockSpec((tm, d), lambda i, f: (i, 0)),
        out_shape=jax.ShapeDtypeStruct((s, d), jnp.float32),
        scratch_shapes=[pltpu.VMEM((tm, d), jnp.float32)],
        compiler_params=_cparams(("parallel", "arbitrary")),
        name="ffn",
    )(h2, w_gate, w_up, w_down, x1, g2, final_w)


def _rope_tables(s):
    n_rows = s // GRID_W
    axis_dim = HEAD_DIM // 2
    inv = ROPE_THETA ** (-jnp.arange(0, axis_dim, 2, dtype=jnp.float32) / axis_dim)
    lane = np.arange(HEAD_DIM)
    inv_lane = inv[(lane // 2) % (axis_dim // 2)]
    by_col = (lane >= axis_dim)[None, :]
    even = (lane % 2 == 0)[None, :]

    def parts(n, mine):
        ang = jnp.arange(n, dtype=jnp.float32)[:, None] * inv_lane[None]
        cos, sin = jnp.cos(ang), jnp.sin(ang)
        tabs = jnp.stack([cos, jnp.where(even, -sin, 0.0), jnp.where(even, 0.0, sin)])
        return jnp.where(mine[None], tabs, 0.0)

    return parts(n_rows, ~by_col), parts(GRID_W, by_col)


def kernel(x, c, w_ada, b_ada, norm1_w, w_in, q_norm_w, k_norm_w, nat_rpb, w_oa, w_ob, w_out, norm2_w,
           w_ffn_gate, w_ffn_up, w_ffn_down, final_w):
    b, s, d = x.shape
    assert b == 1
    depth = w_ada.shape[0]
    n_heads_a = w_oa.shape[1] // HEAD_DIM
    n_heads_b = w_ob.shape[1] // HEAD_DIM
    n_kv_a = n_heads_a // KV_GROUP
    splits = (n_heads_a * HEAD_DIM, n_kv_a * HEAD_DIM, n_kv_a * HEAD_DIM,
              n_heads_b * HEAD_DIM, n_heads_b * HEAD_DIM, n_heads_b * HEAD_DIM, d, d)
    offs = [0] + np.cumsum(splits)[:-1].tolist()
    qa_col, ka_col, va_col, qb_col, kb_col, vb_col, ga_col, gb_col = offs
    assert qa_col == 0

    assert va_col == ka_col + n_kv_a * HEAD_DIM
    rope_rows, rope_cols = _rope_tables(s)
    xs = x[0]
    c_col = c.reshape(d, 1)
    for l in range(depth):
        mod = _adaln(c_col, w_ada[l], b_ada[l][None, :])
        sh1, sc1, g1, sh2, sc2, g2 = [mod[:, i * d:(i + 1) * d] for i in range(6)]
        proj, vt = _inproj(xs, norm1_w[l][None, :], sc1, sh1, w_in[l],
                           q_norm_w[l][None, :], k_norm_w[l][None, :], rope_rows, rope_cols,
                           n_q_heads=n_heads_a, n_k_heads=n_kv_a, qb_cols=(qb_col, kb_col),
                           q_scale=HEAD_DIM ** -0.5 * LOG2E)
        ya = _gqa(proj, vt, n_q_heads=n_heads_a, n_kv_heads=n_kv_a, k_col=ka_col)
        yb, (w_gate, w_up, w_down, w_o, woa_tiles, wob_tiles) = _natten(
            proj, nat_rpb[l],
            [(w_ffn_gate[l], None), (w_ffn_up[l], None), (w_ffn_down[l], None), (w_out[l], None),
             (w_oa[l], MERGE_TILE), (w_ob[l], MERGE_TILE)],
            n_heads=n_heads_b, q_col=qb_col, k_col=kb_col, v_col=vb_col)
        xs, h2 = _merge(ya, yb, proj, woa_tiles, wob_tiles, w_o, xs, g1, norm2_w[l][None, :], sc2, sh2,
                        ga_col=ga_col, gb_col=gb_col)
        xs = _ffn(h2, w_gate, w_up, w_down, xs, g2, final_w[None, :], final_norm=(l == depth - 1))
    return xs[None]
```

```python
import functools

import jax
import jax.numpy as jnp
import numpy as np
from jax import lax
from jax.experimental import pallas as pl
from jax.experimental.pallas import tpu as pltpu

HEAD_DIM = 128
KV_GROUP = 4
GRID_W = 64
NA_ROWS = 8
NA_COLS = 16
ROPE_THETA = 10000.0
NORM_EPS = 1e-6
NEG_BIG = -1e30
LOG2E = 1.4426950408889634
SUBLANES = 8
GQA_KEY_CHUNK = 512
GQA_ONES_ROWS = 16
GQA_STEPS_PER_ITER = 2

VMEM_LIMIT = 56 * 1024 * 1024

_NT = (((1,), (1,)), ((), ()))


def _cparams(sem):
    return pltpu.CompilerParams(dimension_semantics=sem, vmem_limit_bytes=VMEM_LIMIT)


def _adaln_kernel(c_ref, w_ref, b_ref, o_ref):
    c = c_ref[...]
    act = c * jax.nn.sigmoid(c)
    o_ref[...] = jnp.sum(act * w_ref[...], axis=0, keepdims=True) + b_ref[...]


def _adaln(c_col, w_ada, b_ada, tn=1024):
    d, n = w_ada.shape
    return pl.pallas_call(
        _adaln_kernel,
        grid=(n // tn,),
        in_specs=[pl.BlockSpec((d, 1), lambda j: (0, 0)),
                  pl.BlockSpec((d, tn), lambda j: (0, j)),
                  pl.BlockSpec((1, tn), lambda j: (0, j))],
        out_specs=pl.BlockSpec((1, tn), lambda j: (0, j)),
        out_shape=jax.ShapeDtypeStruct((1, n), jnp.float32),
        compiler_params=_cparams(("arbitrary",)),
        name="adaln",
    )(c_col, w_ada, b_ada)


def _rms_modulate(x, w, sc, sh):
    rstd = lax.rsqrt(jnp.mean(x * x, axis=-1, keepdims=True) + NORM_EPS)
    return (x * rstd) * (w * (1.0 + sc)) + sh


def _head_norm_rope(t, w, cos, sin_e, sin_o, scale):
    t = t * lax.rsqrt(jnp.mean(t * t, axis=-1, keepdims=True) + NORM_EPS) * w
    nxt = pltpu.roll(t, HEAD_DIM - 1, 1)
    prv = pltpu.roll(t, 1, 1)
    return (t * cos + nxt * sin_e + prv * sin_o) * scale


def _inproj_kernel(x_ref, n1_ref, sc_ref, sh_ref, w_ref, qw_ref, kw_ref, rt_ref, ct_ref,
                   o_ref, vt_ref, h_ref, rope_ref, *, n_q_tiles, heads_per_tile, n_k_heads, qb_tiles, q_scale):
    j = pl.program_id(1)
    tm = x_ref.shape[0]

    @pl.when(j == 0)
    def _():
        h_ref[...] = _rms_modulate(x_ref[...], n1_ref[...], sc_ref[...], sh_ref[...]).astype(h_ref.dtype)
        for k in range(3):
            for r in range(tm // GRID_W):
                rope_ref[k, r * GRID_W:(r + 1) * GRID_W, :] = rt_ref[k, r:r + 1, :] + ct_ref[k]

    acc = jnp.dot(h_ref[...], w_ref[...].astype(h_ref.dtype), preferred_element_type=jnp.float32)

    def rope_heads(n_heads, w, scale):
        cos, se, so = rope_ref[0], rope_ref[1], rope_ref[2]
        for hh in range(heads_per_tile):
            sl = slice(hh * HEAD_DIM, (hh + 1) * HEAD_DIM)
            t = acc[:, sl]
            if hh < n_heads:
                t = _head_norm_rope(t, w, cos, se, so, scale)
            o_ref[:, sl] = t.astype(o_ref.dtype)

    @pl.when(j < n_q_tiles)
    def _():
        rope_heads(heads_per_tile, qw_ref[...], q_scale)

    @pl.when(j == n_q_tiles)
    def _():
        rope_heads(n_k_heads, kw_ref[...], 1.0)
        n_v, n_chunks, _, tk = vt_ref.shape
        for vh in range(n_v):
            col = (n_k_heads + vh) * HEAD_DIM
            for cc in range(n_chunks):
                v = acc[cc * tk:(cc + 1) * tk, col:col + HEAD_DIM]
                vt_ref[vh, cc, :HEAD_DIM, :] = v.T.astype(vt_ref.dtype)
                vt_ref[vh, cc, HEAD_DIM:, :] = jnp.ones((GQA_ONES_ROWS, tk), vt_ref.dtype)

    in_qb = (j >= qb_tiles[0]) & (j < qb_tiles[1])

    @pl.when(in_qb)
    def _():
        o_ref[...] = (acc * q_scale).astype(o_ref.dtype)

    @pl.when((j > n_q_tiles) & jnp.logical_not(in_qb))
    def _():
        o_ref[...] = acc.astype(o_ref.dtype)


def _inproj(x, n1, sc1, sh1, w_in, qw, kw, rope_rows, rope_cols, *, n_q_heads, n_k_heads, qb_cols, q_scale,
            tm=1024, tn=512, tk=GQA_KEY_CHUNK):
    s, d = x.shape
    n = w_in.shape[1]
    heads_per_tile = tn // HEAD_DIM
    assert n_q_heads % heads_per_tile == 0 and 2 * n_k_heads == heads_per_tile
    assert qb_cols[0] % tn == 0 and qb_cols[1] % tn == 0 and qb_cols[0] // tn > n_q_heads // heads_per_tile
    assert tm % tk == 0 and tm % GRID_W == 0 and (tm // GRID_W) % SUBLANES == 0
    kern = functools.partial(_inproj_kernel, n_q_tiles=n_q_heads // heads_per_tile,
                             heads_per_tile=heads_per_tile, n_k_heads=n_k_heads,
                             qb_tiles=(qb_cols[0] // tn, qb_cols[1] // tn), q_scale=q_scale)
    row = lambda i, j: (0, 0)
    vt_rows = HEAD_DIM + GQA_ONES_ROWS
    return pl.pallas_call(
        kern,
        grid=(s // tm, n // tn),
        in_specs=[pl.BlockSpec((tm, d), lambda i, j: (i, 0)),
                  pl.BlockSpec((1, d), row), pl.BlockSpec((1, d), row), pl.BlockSpec((1, d), row),
                  pl.BlockSpec((d, tn), lambda i, j: (0, j)),
                  pl.BlockSpec((1, HEAD_DIM), row), pl.BlockSpec((1, HEAD_DIM), row),
                  pl.BlockSpec((3, tm // GRID_W, HEAD_DIM), lambda i, j: (0, i, 0)),
                  pl.BlockSpec((3, GRID_W, HEAD_DIM), lambda i, j: (0, 0, 0))],
        out_specs=[pl.BlockSpec((tm, tn), lambda i, j: (i, j)),
                   pl.BlockSpec((n_k_heads, tm // tk, vt_rows, tk), lambda i, j: (0, i, 0, 0))],
        out_shape=[jax.ShapeDtypeStruct((s, n), jnp.bfloat16),
                   jax.ShapeDtypeStruct((n_k_heads, s // tk, vt_rows, tk), jnp.bfloat16)],
        scratch_shapes=[pltpu.VMEM((tm, d), jnp.bfloat16),
                        pltpu.VMEM((3, tm, HEAD_DIM), jnp.float32)],
        compiler_params=_cparams(("parallel", "arbitrary")),
        name="inproj",
    )(x, n1, sc1, sh1, w_in, qw, kw, rope_rows, rope_cols)


def _sublane_allreduce(x, op):
    shift = SUBLANES // 2
    while shift:
        x = op(x, pltpu.roll(x, shift, 0))
        shift //= 2
    return x


def _gqa_kernel(q_ref, qn_ref, k_ref, vt_ref, o_ref, qt_ref, s0_ref, s1_ref, p0_ref, p1_ref, x0_ref, x1_ref,
                a0_ref, a1_ref, m_ref, acc_ref, *, tq, tk, n_chunks):
    s_refs, p_refs, x_refs, a_refs = (s0_ref, s1_ref), (p0_ref, p1_ref), (x0_ref, x1_ref), (a0_ref, a1_ref)
    nq = KV_GROUP * tq
    acc_rows = HEAD_DIM + GQA_ONES_ROWS

    def scores(c, slot):
        k = k_ref[pl.ds(pl.multiple_of(c * tk, tk), tk), :]
        s = jnp.dot(k, qt_ref[...], preferred_element_type=jnp.float32)
        s_refs[slot][...] = s
        x_refs[slot][...] = jnp.max(s.reshape(tk // SUBLANES, SUBLANES, nq), axis=0)

    def softmax(slot):
        m_old = m_ref[...]
        m_new = jnp.maximum(m_old, _sublane_allreduce(x_refs[slot][...], jnp.maximum))
        a_refs[slot][...] = jnp.exp2(m_old - m_new)
        m_ref[...] = m_new
        s = s_refs[slot][...].reshape(tk // SUBLANES, SUBLANES, nq)
        p_refs[slot][...] = jnp.exp2(s - m_new[None]).reshape(tk, nq).astype(p_refs[slot].dtype)

    def weighted_values(c, slot):
        acc = acc_ref[...].reshape(acc_rows // SUBLANES, SUBLANES, nq) * a_refs[slot][...][None]
        acc_ref[...] = acc.reshape(acc_rows, nq) + jnp.dot(
            vt_ref[c], p_refs[slot][...], preferred_element_type=jnp.float32)

    def load_queries(src_ref):
        for g in range(KV_GROUP):
            qg = src_ref[:, g * HEAD_DIM:(g + 1) * HEAD_DIM].astype(jnp.float32)
            qt_ref[:, g * tq:(g + 1) * tq] = qg.T.astype(qt_ref.dtype)

    def reset_and_first_softmax():
        m_ref[...] = jnp.full(m_ref.shape, -jnp.inf, jnp.float32)
        acc_ref[...] = jnp.zeros(acc_ref.shape, jnp.float32)
        softmax(0)

    unroll = GQA_STEPS_PER_ITER
    assert unroll % 2 == 0 and (n_chunks - 2) % unroll == 0

    @pl.when(pl.program_id(1) == 0)
    def _():
        load_queries(q_ref)
        scores(0, 0)
        scores(1, 1)
        reset_and_first_softmax()

    def body(i, carry):
        for u in range(unroll):
            t = unroll * i + 1 + u
            cur = (1 + u) % 2
            scores(t + 1, 1 - cur)
            softmax(cur)
            weighted_values(t - 1, 1 - cur)
        return carry

    lax.fori_loop(0, (n_chunks - 2) // unroll, body, 0)
    load_queries(qn_ref)
    scores(0, 0)
    softmax(1)
    weighted_values(n_chunks - 2, 0)
    scores(1, 1)
    weighted_values(n_chunks - 1, 1)
    acc = acc_ref[...]
    inv_l = 1.0 / acc[HEAD_DIM:HEAD_DIM + SUBLANES, :]
    out_t = (acc[:HEAD_DIM, :].reshape(HEAD_DIM // SUBLANES, SUBLANES, nq) * inv_l[None]).reshape(HEAD_DIM, nq)
    for g in range(KV_GROUP):
        o_ref[:, g * HEAD_DIM:(g + 1) * HEAD_DIM] = out_t[:, g * tq:(g + 1) * tq].T.astype(o_ref.dtype)
    reset_and_first_softmax()


def _gqa(proj, vt, *, n_q_heads, n_kv_heads, k_col, tq=1024):
    s = proj.shape[0]
    _, n_chunks, acc_rows, tk = vt.shape
    gw = KV_GROUP * HEAD_DIM
    nq = KV_GROUP * tq
    n_blocks = s // tq
    kern = functools.partial(_gqa_kernel, tq=tq, tk=tk, n_chunks=n_chunks)
    stat = pltpu.VMEM((SUBLANES, nq), jnp.float32)
    return pl.pallas_call(
        kern,
        grid=(n_kv_heads, n_blocks),
        in_specs=[pl.BlockSpec((tq, gw), lambda h, i: (i, h)),
                  pl.BlockSpec((tq, gw), lambda h, i: (jnp.minimum(i + 1, n_blocks - 1), h)),
                  pl.BlockSpec((s, HEAD_DIM), lambda h, i: (0, k_col // HEAD_DIM + h),
                               pipeline_mode=pl.Buffered(1)),
                  pl.BlockSpec((None, n_chunks, acc_rows, tk), lambda h, i: (h, 0, 0, 0),
                               pipeline_mode=pl.Buffered(1))],
        out_specs=pl.BlockSpec((tq, gw), lambda h, i: (i, h)),
        out_shape=jax.ShapeDtypeStruct((s, n_q_heads * HEAD_DIM), jnp.bfloat16),
        scratch_shapes=[pltpu.VMEM((HEAD_DIM, nq), jnp.bfloat16),
                        pltpu.VMEM((tk, nq), jnp.float32), pltpu.VMEM((tk, nq), jnp.float32),
                        pltpu.VMEM((tk, nq), jnp.bfloat16), pltpu.VMEM((tk, nq), jnp.bfloat16),
                        stat, stat, stat, stat, stat,
                        pltpu.VMEM((acc_rows, nq), jnp.float32)],
        compiler_params=_cparams(("arbitrary", "arbitrary")),
        name="gqa",
    )(proj, proj, proj, vt)


NA_GROUP = 4
NA_WIN_ROWS = NA_GROUP + NA_ROWS


def _na_window_start(r0, n_rows):
    return jnp.clip(r0 - NA_ROWS // 2, 0, n_rows - NA_WIN_ROWS)


def _na_bias_rows(n_rows):
    rel = []
    for r0 in (0, NA_WIN_ROWS, n_rows - NA_GROUP):
        ws = int(np.clip(r0 - NA_ROWS // 2, 0, n_rows - NA_WIN_ROWS))
        per_row = []
        for i in range(NA_GROUP):
            rs = int(np.clip(r0 + i - NA_ROWS // 2, 0, n_rows - NA_ROWS))
            per_row.append([ws + wr - (r0 + i) + NA_ROWS - 1 if rs <= ws + wr < rs + NA_ROWS else None
                            for wr in range(NA_WIN_ROWS)])
        rel.append(per_row)
    return rel


def _na_build_bias(rpb_ref, b_ref, n_rows):
    shape = (GRID_W, 2 * GRID_W)
    lane = lax.broadcasted_iota(jnp.int32, shape, 1)
    c = lax.broadcasted_iota(jnp.int32, shape, 0)
    left = lane < GRID_W
    kc = jnp.where(left, lane, lane - GRID_W)
    cs = jnp.clip(c - NA_COLS // 2, 0, GRID_W - NA_COLS)
    col_ok = (kc >= cs) & (kc < cs + NA_COLS)
    neg = jnp.full(shape, NEG_BIG, jnp.float32)
    n_rel = 2 * NA_ROWS - 1
    tiles = []
    for dr in range(n_rel):
        row = jnp.broadcast_to(rpb_ref[dr:dr + 1, :], shape) * LOG2E
        lo = pltpu.roll(row, 2 * GRID_W - (NA_COLS - 1), 1, stride=1, stride_axis=0)
        hi = pltpu.roll(row, GRID_W - (NA_COLS - 1), 1, stride=1, stride_axis=0)
        tiles.append(jnp.where(col_ok, jnp.where(left, lo, hi), neg))
    rel = _na_bias_rows(n_rows)
    for v in range(3):
        for i in range(NA_GROUP):
            for j in range(NA_WIN_ROWS // 2):
                a, b = rel[v][i][2 * j], rel[v][i][2 * j + 1]
                pair = jnp.where(left, neg if a is None else tiles[a], neg if b is None else tiles[b])
                b_ref[v, i * GRID_W:(i + 1) * GRID_W, j * 2 * GRID_W:(j + 1) * 2 * GRID_W] = pair


def _natten_kernel(q_ref, k_ref, v_ref, rpb_ref, *rest, groups_per_step, n_rows, cast_periods):
    n_cast = len(cast_periods)
    cast_src, o_ref, cast_dst, b_ref = rest[:n_cast], rest[n_cast], rest[n_cast + 1:-1], rest[-1]
    gq = NA_GROUP * GRID_W
    win = NA_WIN_ROWS * GRID_W
    step = pl.program_id(0) * pl.num_programs(1) + pl.program_id(1)
    for src, dst, period in zip(cast_src, cast_dst, cast_periods):
        @pl.when(step % period == 0)
        def _():
            if len(dst.shape) == 3:
                tn = dst.shape[2]
                for n in range(dst.shape[0]):
                    dst[n] = src[:, n * tn:(n + 1) * tn].astype(dst.dtype)
            else:
                dst[...] = src[...].astype(dst.dtype)

    @pl.when(pl.program_id(1) == 0)
    def _():
        _na_build_bias(rpb_ref, b_ref, n_rows)

    for g in range(groups_per_step):
        r0 = (pl.program_id(1) * groups_per_step + g) * NA_GROUP
        variant = jnp.where(r0 == 0, 0, jnp.where(r0 == n_rows - NA_GROUP, 2, 1))
        k0 = pl.multiple_of(_na_window_start(r0, n_rows) * GRID_W, GRID_W)
        q = q_ref[g * gq:(g + 1) * gq, :]
        kw = k_ref[pl.ds(k0, win), :]
        vw = v_ref[pl.ds(k0, win), :]
        s = lax.dot_general(q, kw, _NT, preferred_element_type=jnp.float32) + b_ref[variant]
        p = jnp.exp2(s - jnp.max(s, axis=-1, keepdims=True))
        l = jnp.sum(p, axis=-1, keepdims=True)
        o = jnp.dot(p.astype(vw.dtype), vw, preferred_element_type=jnp.float32) / l
        o_ref[g * gq:(g + 1) * gq, :] = o.astype(o_ref.dtype)


def _cast_plan(w, tile_cols, n_steps, step_of):
    rows, cols = w.shape
    n_blocks = next(nb for nb in range(n_steps, 0, -1)
                    if n_steps % nb == 0 and rows % nb == 0 and (rows // nb) % (2 * SUBLANES) == 0)
    rb, period = rows // n_blocks, n_steps // n_blocks
    src = pl.BlockSpec((rb, cols), lambda h, i: (step_of(h, i) // period, 0))
    if tile_cols is None:
        dst = pl.BlockSpec((rb, cols), lambda h, i: (step_of(h, i) // period, 0))
        shape = jax.ShapeDtypeStruct((rows, cols), jnp.bfloat16)
    else:
        n_tiles = cols // tile_cols
        dst = pl.BlockSpec((n_tiles, rb, tile_cols), lambda h, i: (0, step_of(h, i) // period, 0))
        shape = jax.ShapeDtypeStruct((n_tiles, rows, tile_cols), jnp.bfloat16)
    return src, dst, shape, period


def _natten(proj, rpb, casts, *, n_heads, q_col, k_col, v_col, groups_per_step=16):
    s = proj.shape[0]
    n_rows = s // GRID_W
    tq = groups_per_step * NA_GROUP * GRID_W
    gq, win = NA_GROUP * GRID_W, NA_WIN_ROWS * GRID_W
    assert n_rows % (groups_per_step * NA_GROUP) == 0 and n_rows >= 2 * NA_WIN_ROWS
    assert NA_WIN_ROWS % 2 == 0 and 2 * GRID_W == HEAD_DIM and rpb.shape[2] <= GRID_W
    rel_rows = -(-rpb.shape[1] // SUBLANES) * SUBLANES
    rpb_rows = jnp.pad(rpb, ((0, 0), (0, rel_rows - rpb.shape[1]), (0, 2 * GRID_W - rpb.shape[2])))
    n_i = s // tq
    plans = [_cast_plan(w, tile_cols, n_heads * n_i, lambda h, i: h * n_i + i) for w, tile_cols in casts]
    kern = functools.partial(_natten_kernel, groups_per_step=groups_per_step, n_rows=n_rows,
                             cast_periods=tuple(p[3] for p in plans))
    out = pl.pallas_call(
        kern,
        grid=(n_heads, n_i),
        in_specs=[pl.BlockSpec((tq, HEAD_DIM), lambda h, i: (i, q_col // HEAD_DIM + h)),
                  pl.BlockSpec((s, HEAD_DIM), lambda h, i: (0, k_col // HEAD_DIM + h)),
                  pl.BlockSpec((s, HEAD_DIM), lambda h, i: (0, v_col // HEAD_DIM + h)),
                  pl.BlockSpec((None, rel_rows, 2 * GRID_W), lambda h, i: (h, 0, 0))] + [p[0] for p in plans],
        out_specs=[pl.BlockSpec((tq, HEAD_DIM), lambda h, i: (i, h))] + [p[1] for p in plans],
        out_shape=[jax.ShapeDtypeStruct((s, n_heads * HEAD_DIM), jnp.bfloat16)] + [p[2] for p in plans],
        scratch_shapes=[pltpu.VMEM((3, gq, win), jnp.float32)],
        compiler_params=_cparams(("arbitrary", "arbitrary")),
        name="natten",
    )(proj, proj, proj, rpb_rows, *[w for w, _ in casts])
    return out[0], out[1:]


def _merge_kernel(ya_ref, yb_ref, ga_ref, gb_ref, woa_ref, wob_ref, wout_ref, x_ref, g1_ref, n2_ref, sc_ref,
                  sh_ref, x1_ref, h2_ref, m_ref, *, tn, n_tiles):
    n = pl.program_id(1)
    a = jnp.dot(ya_ref[...], woa_ref[n], preferred_element_type=jnp.float32)
    b = jnp.dot(yb_ref[...], wob_ref[n], preferred_element_type=jnp.float32)
    m = (jax.nn.sigmoid(ga_ref[...].astype(jnp.float32)) * a
         + jax.nn.sigmoid(gb_ref[...].astype(jnp.float32)) * b)
    col = pl.multiple_of(n * tn, tn)
    m_ref[:, pl.ds(col, tn)] = m.astype(m_ref.dtype)

    @pl.when(n == n_tiles - 1)
    def _():
        y = jnp.dot(m_ref[...], wout_ref[...], preferred_element_type=jnp.float32)
        x1 = x_ref[...] + g1_ref[...] * y
        x1_ref[...] = x1
        h2_ref[...] = _rms_modulate(x1, n2_ref[...], sc_ref[...], sh_ref[...]).astype(h2_ref.dtype)


MERGE_TILE = 512


def _merge(ya, yb, proj, woa_tiles, wob_tiles, w_out, x, g1, n2, sc2, sh2, *, ga_col, gb_col, tm=512):
    s, d = x.shape
    da = ya.shape[1]
    n_tiles, _, tn = woa_tiles.shape
    kern = functools.partial(_merge_kernel, tn=tn, n_tiles=n_tiles)
    row = lambda i, n: (0, 0)
    tiles = pl.BlockSpec((n_tiles, da, tn), lambda i, n: (0, 0, 0), pipeline_mode=pl.Buffered(1))
    return pl.pallas_call(
        kern,
        grid=(s // tm, n_tiles),
        in_specs=[pl.BlockSpec((tm, da), lambda i, n: (i, 0)),
                  pl.BlockSpec((tm, da), lambda i, n: (i, 0)),
                  pl.BlockSpec((tm, tn), lambda i, n: (i, ga_col // tn + n)),
                  pl.BlockSpec((tm, tn), lambda i, n: (i, gb_col // tn + n)),
                  tiles, tiles,
                  pl.BlockSpec((d, d), row, pipeline_mode=pl.Buffered(1)),
                  pl.BlockSpec((tm, d), lambda i, n: (i, 0)),
                  pl.BlockSpec((1, d), row), pl.BlockSpec((1, d), row),
                  pl.BlockSpec((1, d), row), pl.BlockSpec((1, d), row)],
        out_specs=[pl.BlockSpec((tm, d), lambda i, n: (i, 0)),
                   pl.BlockSpec((tm, d), lambda i, n: (i, 0))],
        out_shape=[jax.ShapeDtypeStruct((s, d), jnp.float32),
                   jax.ShapeDtypeStruct((s, d), jnp.bfloat16)],
        scratch_shapes=[pltpu.VMEM((tm, d), jnp.bfloat16)],
        compiler_params=_cparams(("parallel", "arbitrary")),
        name="merge",
    )(ya, yb, proj, proj, woa_tiles, wob_tiles, w_out, x, g1, n2, sc2, sh2)


def _ffn_kernel(h_ref, wg_ref, wu_ref, wd_ref, x1_ref, g2_ref, fw_ref, o_ref, acc_ref, *, n_tiles, final_norm):
    f = pl.program_id(1)

    @pl.when(f == 0)
    def _():
        acc_ref[...] = jnp.zeros(acc_ref.shape, jnp.float32)

    h = h_ref[...]
    g = jnp.dot(h, wg_ref[...], preferred_element_type=jnp.float32)
    u = jnp.dot(h, wu_ref[...], preferred_element_type=jnp.float32)
    a = (g * jax.nn.sigmoid(g) * u).astype(h.dtype)
    acc_ref[...] += jnp.dot(a, wd_ref[...], preferred_element_type=jnp.float32)

    @pl.when(f == n_tiles - 1)
    def _():
        x2 = x1_ref[...] + g2_ref[...] * acc_ref[...]
        if final_norm:
            x2 = x2 * lax.rsqrt(jnp.mean(x2 * x2, axis=-1, keepdims=True) + NORM_EPS) * fw_ref[...]
        o_ref[...] = x2


def _ffn(h2, w_gate, w_up, w_down, x1, g2, final_w, *, final_norm, tm=512, tf=512):
    s, d = x1.shape
    dff = w_gate.shape[1]
    n_tiles = dff // tf
    kern = functools.partial(_ffn_kernel, n_tiles=n_tiles, final_norm=final_norm)
    row = lambda i, f: (0, 0)
    return pl.pallas_call(
        kern,
        grid=(s // tm, n_tiles),
        in_specs=[pl.BlockSpec((tm, d), lambda i, f: (i, 0)),
                  pl.BlockSpec((d, tf), lambda i, f: (0, f)),
                  pl.BlockSpec((d, tf), lambda i, f: (0, f)),
                  pl.BlockSpec((tf, d), lambda i, f: (f, 0)),
                  pl.BlockSpec((tm, d), lambda i, f: (i, 0)),
                  pl.BlockSpec((1, d), row), pl.BlockSpec((1, d), row)],
        out_specs=pl.BlockSpec((tm, d), lambda i, f: (i, 0)),
        out_shape=jax.ShapeDtypeStruct((s, d), jnp.float32),
        scratch_shapes=[pltpu.VMEM((tm, d), jnp.float32)],
        compiler_params=_cparams(("parallel", "arbitrary")),
        name="ffn",
    )(h2, w_gate, w_up, w_down, x1, g2, final_w)


def _rope_tables(s):
    n_rows = s // GRID_W
    axis_dim = HEAD_DIM // 2
    inv = ROPE_THETA ** (-jnp.arange(0, axis_dim, 2, dtype=jnp.float32) / axis_dim)
    lane = np.arange(HEAD_DIM)
    inv_lane = inv[(lane // 2) % (axis_dim // 2)]
    by_col = (lane >= axis_dim)[None, :]
    even = (lane % 2 == 0)[None, :]

    def parts(n, mine):
        ang = jnp.arange(n, dtype=jnp.float32)[:, None] * inv_lane[None]
        cos, sin = jnp.cos(ang), jnp.sin(ang)
        tabs = jnp.stack([cos, jnp.where(even, -sin, 0.0), jnp.where(even, 0.0, sin)])
        return jnp.where(mine[None], tabs, 0.0)

    return parts(n_rows, ~by_col), parts(GRID_W, by_col)


def kernel(x, c, w_ada, b_ada, norm1_w, w_in, q_norm_w, k_norm_w, nat_rpb, w_oa, w_ob, w_out, norm2_w,
           w_ffn_gate, w_ffn_up, w_ffn_down, final_w):
    b, s, d = x.shape
    assert b == 1
    depth = w_ada.shape[0]
    n_heads_a = w_oa.shape[1] // HEAD_DIM
    n_heads_b = w_ob.shape[1] // HEAD_DIM
    n_kv_a = n_heads_a // KV_GROUP
    splits = (n_heads_a * HEAD_DIM, n_kv_a * HEAD_DIM, n_kv_a * HEAD_DIM,
              n_heads_b * HEAD_DIM, n_heads_b * HEAD_DIM, n_heads_b * HEAD_DIM, d, d)
    offs = [0] + np.cumsum(splits)[:-1].tolist()
    qa_col, ka_col, va_col, qb_col, kb_col, vb_col, ga_col, gb_col = offs
    assert qa_col == 0

    assert va_col == ka_col + n_kv_a * HEAD_DIM
    rope_rows, rope_cols = _rope_tables(s)
    xs = x[0]
    c_col = c.reshape(d, 1)
    for l in range(depth):
        mod = _adaln(c_col, w_ada[l], b_ada[l][None, :])
        sh1, sc1, g1, sh2, sc2, g2 = [mod[:, i * d:(i + 1) * d] for i in range(6)]
        proj, vt = _inproj(xs, norm1_w[l][None, :], sc1, sh1, w_in[l],
                           q_norm_w[l][None, :], k_norm_w[l][None, :], rope_rows, rope_cols,
                           n_q_heads=n_heads_a, n_k_heads=n_kv_a, qb_cols=(qb_col, kb_col),
                           q_scale=HEAD_DIM ** -0.5 * LOG2E)
        ya = _gqa(proj, vt, n_q_heads=n_heads_a, n_kv_heads=n_kv_a, k_col=ka_col)
        yb, (w_gate, w_up, w_down, w_o, woa_tiles, wob_tiles) = _natten(
            proj, nat_rpb[l],
            [(w_ffn_gate[l], None), (w_ffn_up[l], None), (w_ffn_down[l], None), (w_out[l], None),
             (w_oa[l], MERGE_TILE), (w_ob[l], MERGE_TILE)],
            n_heads=n_heads_b, q_col=qb_col, k_col=kb_col, v_col=vb_col)
        xs, h2 = _merge(ya, yb, proj, woa_tiles, wob_tiles, w_o, xs, g1, norm2_w[l][None, :], sc2, sh2,
                        ga_col=ga_col, gb_col=gb_col)
        xs = _ffn(h2, w_gate, w_up, w_down, xs, g2, final_w[None, :], final_norm=(l == depth - 1))
    return xs[None]
```

```python
import functools

import jax
import jax.numpy as jnp
import numpy as np
from jax import lax
from jax.experimental import pallas as pl
from jax.experimental.pallas import tpu as pltpu

HEAD_DIM = 128
KV_GROUP = 4
GRID_W = 64
NA_ROWS = 8
NA_COLS = 16
ROPE_THETA = 10000.0
NORM_EPS = 1e-6
NEG_BIG = -1e30
LOG2E = 1.4426950408889634
SUBLANES = 8
GQA_KEY_CHUNK = 512
GQA_ONES_ROWS = 16
GQA_STEPS_PER_ITER = 2

VMEM_LIMIT = 56 * 1024 * 1024

_NT = (((1,), (1,)), ((), ()))


def _cparams(sem):
    return pltpu.CompilerParams(dimension_semantics=sem, vmem_limit_bytes=VMEM_LIMIT)


def _adaln_kernel(c_ref, w_ref, b_ref, o_ref):
    c = c_ref[...]
    act = c * jax.nn.sigmoid(c)
    o_ref[...] = jnp.sum(act * w_ref[...], axis=0, keepdims=True) + b_ref[...]


def _adaln(c_col, w_ada, b_ada, tn=1024):
    d, n = w_ada.shape
    return pl.pallas_call(
        _adaln_kernel,
        grid=(n // tn,),
        in_specs=[pl.BlockSpec((d, 1), lambda j: (0, 0)),
                  pl.BlockSpec((d, tn), lambda j: (0, j)),
                  pl.BlockSpec((1, tn), lambda j: (0, j))],
        out_specs=pl.BlockSpec((1, tn), lambda j: (0, j)),
        out_shape=jax.ShapeDtypeStruct((1, n), jnp.float32),
        compiler_params=_cparams(("arbitrary",)),
        name="adaln",
    )(c_col, w_ada, b_ada)


def _rms_modulate(x, w, sc, sh):
    rstd = lax.rsqrt(jnp.mean(x * x, axis=-1, keepdims=True) + NORM_EPS)
    return (x * rstd) * (w * (1.0 + sc)) + sh


def _head_norm_rope(t, w, cos, sin_e, sin_o, scale):
    t = t * lax.rsqrt(jnp.mean(t * t, axis=-1, keepdims=True) + NORM_EPS) * w
    nxt = pltpu.roll(t, HEAD_DIM - 1, 1)
    prv = pltpu.roll(t, 1, 1)
    return (t * cos + nxt * sin_e + prv * sin_o) * scale


def _inproj_kernel(x_ref, n1_ref, sc_ref, sh_ref, w_ref, qw_ref, kw_ref, rt_ref, ct_ref,
                   o_ref, vt_ref, h_ref, rope_ref, *, n_q_tiles, heads_per_tile, n_k_heads, qb_tiles, q_scale):
    j = pl.program_id(1)
    tm = x_ref.shape[0]

    @pl.when(j == 0)
    def _():
        h_ref[...] = _rms_modulate(x_ref[...], n1_ref[...], sc_ref[...], sh_ref[...]).astype(h_ref.dtype)
        for k in range(3):
            for r in range(tm // GRID_W):
                rope_ref[k, r * GRID_W:(r + 1) * GRID_W, :] = rt_ref[k, r:r + 1, :] + ct_ref[k]

    acc = jnp.dot(h_ref[...], w_ref[...].astype(h_ref.dtype), preferred_element_type=jnp.float32)

    def rope_heads(n_heads, w, scale):
        cos, se, so = rope_ref[0], rope_ref[1], rope_ref[2]
        for hh in range(heads_per_tile):
            sl = slice(hh * HEAD_DIM, (hh + 1) * HEAD_DIM)
            t = acc[:, sl]
            if hh < n_heads:
                t = _head_norm_rope(t, w, cos, se, so, scale)
            o_ref[:, sl] = t.astype(o_ref.dtype)

    @pl.when(j < n_q_tiles)
    def _():
        rope_heads(heads_per_tile, qw_ref[...], q_scale)

    @pl.when(j == n_q_tiles)
    def _():
        rope_heads(n_k_heads, kw_ref[...], 1.0)
        n_v, n_chunks, _, tk = vt_ref.shape
        for vh in range(n_v):
            col = (n_k_heads + vh) * HEAD_DIM
            for cc in range(n_chunks):
                v = acc[cc * tk:(cc + 1) * tk, col:col + HEAD_DIM]
                vt_ref[vh, cc, :HEAD_DIM, :] = v.T.astype(vt_ref.dtype)
                vt_ref[vh, cc, HEAD_DIM:, :] = jnp.ones((GQA_ONES_ROWS, tk), vt_ref.dtype)

    in_qb = (j >= qb_tiles[0]) & (j < qb_tiles[1])

    @pl.when(in_qb)
    def _():
        o_ref[...] = (acc * q_scale).astype(o_ref.dtype)

    @pl.when((j > n_q_tiles) & jnp.logical_not(in_qb))
    def _():
        o_ref[...] = acc.astype(o_ref.dtype)


def _inproj(x, n1, sc1, sh1, w_in, qw, kw, rope_rows, rope_cols, *, n_q_heads, n_k_heads, qb_cols, q_scale,
            tm=1024, tn=512, tk=GQA_KEY_CHUNK):
    s, d = x.shape
    n = w_in.shape[1]
    heads_per_tile = tn // HEAD_DIM
    assert n_q_heads % heads_per_tile == 0 and 2 * n_k_heads == heads_per_tile
    assert qb_cols[0] % tn == 0 and qb_cols[1] % tn == 0 and qb_cols[0] // tn > n_q_heads // heads_per_tile
    assert tm % tk == 0 and tm % GRID_W == 0 and (tm // GRID_W) % SUBLANES == 0
    kern = functools.partial(_inproj_kernel, n_q_tiles=n_q_heads // heads_per_tile,
                             heads_per_tile=heads_per_tile, n_k_heads=n_k_heads,
                             qb_tiles=(qb_cols[0] // tn, qb_cols[1] // tn), q_scale=q_scale)
    row = lambda i, j: (0, 0)
    vt_rows = HEAD_DIM + GQA_ONES_ROWS
    return pl.pallas_call(
        kern,
        grid=(s // tm, n // tn),
        in_specs=[pl.BlockSpec((tm, d), lambda i, j: (i, 0)),
                  pl.BlockSpec((1, d), row), pl.BlockSpec((1, d), row), pl.BlockSpec((1, d), row),
                  pl.BlockSpec((d, tn), lambda i, j: (0, j)),
                  pl.BlockSpec((1, HEAD_DIM), row), pl.BlockSpec((1, HEAD_DIM), row),
                  pl.BlockSpec((3, tm // GRID_W, HEAD_DIM), lambda i, j: (0, i, 0)),
                  pl.BlockSpec((3, GRID_W, HEAD_DIM), lambda i, j: (0, 0, 0))],
        out_specs=[pl.BlockSpec((tm, tn), lambda i, j: (i, j)),
                   pl.BlockSpec((n_k_heads, tm // tk, vt_rows, tk), lambda i, j: (0, i, 0, 0))],
        out_shape=[jax.ShapeDtypeStruct((s, n), jnp.bfloat16),
                   jax.ShapeDtypeStruct((n_k_heads, s // tk, vt_rows, tk), jnp.bfloat16)],
        scratch_shapes=[pltpu.VMEM((tm, d), jnp.bfloat16),
                        pltpu.VMEM((3, tm, HEAD_DIM), jnp.float32)],
        compiler_params=_cparams(("parallel", "arbitrary")),
        name="inproj",
    )(x, n1, sc1, sh1, w_in, qw, kw, rope_rows, rope_cols)


def _sublane_allreduce(x, op):
    shift = SUBLANES // 2
    while shift:
        x = op(x, pltpu.roll(x, shift, 0))
        shift //= 2
    return x


def _gqa_kernel(q_ref, qn_ref, k_ref, vt_ref, o_ref, qt_ref, s0_ref, s1_ref, p0_ref, p1_ref, x0_ref, x1_ref,
                a0_ref, a1_ref, m_ref, acc_ref, *, tq, tk, n_chunks):
    s_refs, p_refs, x_refs, a_refs = (s0_ref, s1_ref), (p0_ref, p1_ref), (x0_ref, x1_ref), (a0_ref, a1_ref)
    nq = KV_GROUP * tq
    acc_rows = HEAD_DIM + GQA_ONES_ROWS

    def scores(c, slot):
        k = k_ref[pl.ds(pl.multiple_of(c * tk, tk), tk), :]
        s = jnp.dot(k, qt_ref[...], preferred_element_type=jnp.float32)
        s_refs[slot][...] = s
        x_refs[slot][...] = jnp.max(s.reshape(tk // SUBLANES, SUBLANES, nq), axis=0)

    def softmax(slot):
        m_old = m_ref[...]
        m_new = jnp.maximum(m_old, _sublane_allreduce(x_refs[slot][...], jnp.maximum))
        a_refs[slot][...] = jnp.exp2(m_old - m_new)
        m_ref[...] = m_new
        s = s_refs[slot][...].reshape(tk // SUBLANES, SUBLANES, nq)
        p_refs[slot][...] = jnp.exp2(s - m_new[None]).reshape(tk, nq).astype(p_refs[slot].dtype)

    def weighted_values(c, slot):
        acc = acc_ref[...].reshape(acc_rows // SUBLANES, SUBLANES, nq) * a_refs[slot][...][None]
        acc_ref[...] = acc.reshape(acc_rows, nq) + jnp.dot(
            vt_ref[c], p_refs[slot][...], preferred_element_type=jnp.float32)

    def load_queries(src_ref):
        for g in range(KV_GROUP):
            qg = src_ref[:, g * HEAD_DIM:(g + 1) * HEAD_DIM].astype(jnp.float32)
            qt_ref[:, g * tq:(g + 1) * tq] = qg.T.astype(qt_ref.dtype)

    def reset_and_first_softmax():
        m_ref[...] = jnp.full(m_ref.shape, -jnp.inf, jnp.float32)
        acc_ref[...] = jnp.zeros(acc_ref.shape, jnp.float32)
        softmax(0)

    unroll = GQA_STEPS_PER_ITER
    assert unroll % 2 == 0 and (n_chunks - 2) % unroll == 0

    @pl.when(pl.program_id(1) == 0)
    def _():
        load_queries(q_ref)
        scores(0, 0)
        scores(1, 1)
        reset_and_first_softmax()

    def body(i, carry):
        for u in range(unroll):
            t = unroll * i + 1 + u
            cur = (1 + u) % 2
            scores(t + 1, 1 - cur)
            softmax(cur)
            weighted_values(t - 1, 1 - cur)
        return carry

    lax.fori_loop(0, (n_chunks - 2) // unroll, body, 0)
    load_queries(qn_ref)
    scores(0, 0)
    softmax(1)
    weighted_values(n_chunks - 2, 0)
    scores(1, 1)
    weighted_values(n_chunks - 1, 1)
    acc = acc_ref[...]
    inv_l = 1.0 / acc[HEAD_DIM:HEAD_DIM + SUBLANES, :]
    out_t = (acc[:HEAD_DIM, :].reshape(HEAD_DIM // SUBLANES, SUBLANES, nq) * inv_l[None]).reshape(HEAD_DIM, nq)
    for g in range(KV_GROUP):
        o_ref[:, g * HEAD_DIM:(g + 1) * HEAD_DIM] = out_t[:, g * tq:(g + 1) * tq].T.astype(o_ref.dtype)
    reset_and_first_softmax()


def _gqa(proj, vt, *, n_q_heads, n_kv_heads, k_col, tq=1024):
    s = proj.shape[0]
    _, n_chunks, acc_rows, tk = vt.shape
    gw = KV_GROUP * HEAD_DIM
    nq = KV_GROUP * tq
    n_blocks = s // tq
    kern = functools.partial(_gqa_kernel, tq=tq, tk=tk, n_chunks=n_chunks)
    stat = pltpu.VMEM((SUBLANES, nq), jnp.float32)
    return pl.pallas_call(
        kern,
        grid=(n_kv_heads, n_blocks),
        in_specs=[pl.BlockSpec((tq, gw), lambda h, i: (i, h)),
                  pl.BlockSpec((tq, gw), lambda h, i: (jnp.minimum(i + 1, n_blocks - 1), h)),
                  pl.BlockSpec((s, HEAD_DIM), lambda h, i: (0, k_col // HEAD_DIM + h),
                               pipeline_mode=pl.Buffered(1)),
                  pl.BlockSpec((None, n_chunks, acc_rows, tk), lambda h, i: (h, 0, 0, 0),
                               pipeline_mode=pl.Buffered(1))],
        out_specs=pl.BlockSpec((tq, gw), lambda h, i: (i, h)),
        out_shape=jax.ShapeDtypeStruct((s, n_q_heads * HEAD_DIM), jnp.bfloat16),
        scratch_shapes=[pltpu.VMEM((HEAD_DIM, nq), jnp.bfloat16),
                        pltpu.VMEM((tk, nq), jnp.float32), pltpu.VMEM((tk, nq), jnp.float32),
                        pltpu.VMEM((tk, nq), jnp.bfloat16), pltpu.VMEM((tk, nq), jnp.bfloat16),
                        stat, stat, stat, stat, stat,
                        pltpu.VMEM((acc_rows, nq), jnp.float32)],
        compiler_params=_cparams(("arbitrary", "arbitrary")),
        name="gqa",
    )(proj, proj, proj, vt)


NA_GROUP = 4
NA_WIN_ROWS = NA_GROUP + NA_ROWS


def _na_window_start(r0, n_rows):
    return jnp.clip(r0 - NA_ROWS // 2, 0, n_rows - NA_WIN_ROWS)


def _na_bias_rows(n_rows):
    rel = []
    for r0 in (0, NA_WIN_ROWS, n_rows - NA_GROUP):
        ws = int(np.clip(r0 - NA_ROWS // 2, 0, n_rows - NA_WIN_ROWS))
        per_row = []
        for i in range(NA_GROUP):
            rs = int(np.clip(r0 + i - NA_ROWS // 2, 0, n_rows - NA_ROWS))
            per_row.append([ws + wr - (r0 + i) + NA_ROWS - 1 if rs <= ws + wr < rs + NA_ROWS else None
                            for wr in range(NA_WIN_ROWS)])
        rel.append(per_row)
    return rel


def _na_build_bias(rpb_ref, b_ref, n_rows):
    shape = (GRID_W, 2 * GRID_W)
    lane = lax.broadcasted_iota(jnp.int32, shape, 1)
    c = lax.broadcasted_iota(jnp.int32, shape, 0)
    left = lane < GRID_W
    kc = jnp.where(left, lane, lane - GRID_W)
    cs = jnp.clip(c - NA_COLS // 2, 0, GRID_W - NA_COLS)
    col_ok = (kc >= cs) & (kc < cs + NA_COLS)
    neg = jnp.full(shape, NEG_BIG, jnp.float32)
    n_rel = 2 * NA_ROWS - 1
    tiles = []
    for dr in range(n_rel):
        row = jnp.broadcast_to(rpb_ref[dr:dr + 1, :], shape) * LOG2E
        lo = pltpu.roll(row, 2 * GRID_W - (NA_COLS - 1), 1, stride=1, stride_axis=0)
        hi = pltpu.roll(row, GRID_W - (NA_COLS - 1), 1, stride=1, stride_axis=0)
        tiles.append(jnp.where(col_ok, jnp.where(left, lo, hi), neg))
    rel = _na_bias_rows(n_rows)
    for v in range(3):
        for i in range(NA_GROUP):
            for j in range(NA_WIN_ROWS // 2):
                a, b = rel[v][i][2 * j], rel[v][i][2 * j + 1]
                pair = jnp.where(left, neg if a is None else tiles[a], neg if b is None else tiles[b])
                b_ref[v, i * GRID_W:(i + 1) * GRID_W, j * 2 * GRID_W:(j + 1) * 2 * GRID_W] = pair


def _natten_kernel(q_ref, k_ref, v_ref, rpb_ref, *rest, groups_per_step, n_rows, cast_periods):
    n_cast = len(cast_periods)
    cast_src, o_ref, cast_dst, b_ref = rest[:n_cast], rest[n_cast], rest[n_cast + 1:-1], rest[-1]
    gq = NA_GROUP * GRID_W
    win = NA_WIN_ROWS * GRID_W
    step = pl.program_id(0) * pl.num_programs(1) + pl.program_id(1)
    for src, dst, period in zip(cast_src, cast_dst, cast_periods):
        @pl.when(step % period == 0)
        def _():
            if len(dst.shape) == 3:
                tn = dst.shape[2]
                for n in range(dst.shape[0]):
                    dst[n] = src[:, n * tn:(n + 1) * tn].astype(dst.dtype)
            else:
                dst[...] = src[...].astype(dst.dtype)

    @pl.when(pl.program_id(1) == 0)
    def _():
        _na_build_bias(rpb_ref, b_ref, n_rows)

    for g in range(groups_per_step):
        r0 = (pl.program_id(1) * groups_per_step + g) * NA_GROUP
        variant = jnp.where(r0 == 0, 0, jnp.where(r0 == n_rows - NA_GROUP, 2, 1))
        k0 = pl.multiple_of(_na_window_start(r0, n_rows) * GRID_W, GRID_W)
        q = q_ref[g * gq:(g + 1) * gq, :]
        kw = k_ref[pl.ds(k0, win), :]
        vw = v_ref[pl.ds(k0, win), :]
        s = lax.dot_general(q, kw, _NT, preferred_element_type=jnp.float32) + b_ref[variant]
        p = jnp.exp2(s - jnp.max(s, axis=-1, keepdims=True))
        v1 = jnp.concatenate([vw, jnp.ones_like(vw)], axis=1)
        o = jnp.dot(p.astype(vw.dtype), v1, preferred_element_type=jnp.float32)
        o_ref[g * gq:(g + 1) * gq, :] = (o[:, :HEAD_DIM] / o[:, HEAD_DIM:]).astype(o_ref.dtype)


def _cast_plan(w, tile_cols, n_steps, step_of):
    rows, cols = w.shape
    n_blocks = next(nb for nb in range(n_steps, 0, -1)
                    if n_steps % nb == 0 and rows % nb == 0 and (rows // nb) % (2 * SUBLANES) == 0)
    rb, period = rows // n_blocks, n_steps // n_blocks
    src = pl.BlockSpec((rb, cols), lambda h, i: (step_of(h, i) // period, 0))
    if tile_cols is None:
        dst = pl.BlockSpec((rb, cols), lambda h, i: (step_of(h, i) // period, 0))
        shape = jax.ShapeDtypeStruct((rows, cols), jnp.bfloat16)
    else:
        n_tiles = cols // tile_cols
        dst = pl.BlockSpec((n_tiles, rb, tile_cols), lambda h, i: (0, step_of(h, i) // period, 0))
        shape = jax.ShapeDtypeStruct((n_tiles, rows, tile_cols), jnp.bfloat16)
    return src, dst, shape, period


def _natten(proj, rpb, casts, *, n_heads, q_col, k_col, v_col, groups_per_step=16):
    s = proj.shape[0]
    n_rows = s // GRID_W
    tq = groups_per_step * NA_GROUP * GRID_W
    gq, win = NA_GROUP * GRID_W, NA_WIN_ROWS * GRID_W
    assert n_rows % (groups_per_step * NA_GROUP) == 0 and n_rows >= 2 * NA_WIN_ROWS
    assert NA_WIN_ROWS % 2 == 0 and 2 * GRID_W == HEAD_DIM and rpb.shape[2] <= GRID_W
    rel_rows = -(-rpb.shape[1] // SUBLANES) * SUBLANES
    rpb_rows = jnp.pad(rpb, ((0, 0), (0, rel_rows - rpb.shape[1]), (0, 2 * GRID_W - rpb.shape[2])))
    n_i = s // tq
    plans = [_cast_plan(w, tile_cols, n_heads * n_i, lambda h, i: h * n_i + i) for w, tile_cols in casts]
    kern = functools.partial(_natten_kernel, groups_per_step=groups_per_step, n_rows=n_rows,
                             cast_periods=tuple(p[3] for p in plans))
    out = pl.pallas_call(
        kern,
        grid=(n_heads, n_i),
        in_specs=[pl.BlockSpec((tq, HEAD_DIM), lambda h, i: (i, q_col // HEAD_DIM + h)),
                  pl.BlockSpec((s, HEAD_DIM), lambda h, i: (0, k_col // HEAD_DIM + h)),
                  pl.BlockSpec((s, HEAD_DIM), lambda h, i: (0, v_col // HEAD_DIM + h)),
                  pl.BlockSpec((None, rel_rows, 2 * GRID_W), lambda h, i: (h, 0, 0))] + [p[0] for p in plans],
        out_specs=[pl.BlockSpec((tq, HEAD_DIM), lambda h, i: (i, h))] + [p[1] for p in plans],
        out_shape=[jax.ShapeDtypeStruct((s, n_heads * HEAD_DIM), jnp.bfloat16)] + [p[2] for p in plans],
        scratch_shapes=[pltpu.VMEM((3, gq, win), jnp.float32)],
        compiler_params=_cparams(("arbitrary", "arbitrary")),
        name="natten",
    )(proj, proj, proj, rpb_rows, *[w for w, _ in casts])
    return out[0], out[1:]


def _merge_kernel(ya_ref, yb_ref, ga_ref, gb_ref, woa_ref, wob_ref, wout_ref, x_ref, g1_ref, n2_ref, sc_ref,
                  sh_ref, x1_ref, h2_ref, m_ref, *, tn, n_tiles):
    n = pl.program_id(1)
    a = jnp.dot(ya_ref[...], woa_ref[n], preferred_element_type=jnp.float32)
    b = jnp.dot(yb_ref[...], wob_ref[n], preferred_element_type=jnp.float32)
    m = (jax.nn.sigmoid(ga_ref[...].astype(jnp.float32)) * a
         + jax.nn.sigmoid(gb_ref[...].astype(jnp.float32)) * b)
    col = pl.multiple_of(n * tn, tn)
    m_ref[:, pl.ds(col, tn)] = m.astype(m_ref.dtype)

    @pl.when(n == n_tiles - 1)
    def _():
        y = jnp.dot(m_ref[...], wout_ref[...], preferred_element_type=jnp.float32)
        x1 = x_ref[...] + g1_ref[...] * y
        x1_ref[...] = x1
        h2_ref[...] = _rms_modulate(x1, n2_ref[...], sc_ref[...], sh_ref[...]).astype(h2_ref.dtype)


MERGE_TILE = 512


def _merge(ya, yb, proj, woa_tiles, wob_tiles, w_out, x, g1, n2, sc2, sh2, *, ga_col, gb_col, tm=512):
    s, d = x.shape
    da = ya.shape[1]
    n_tiles, _, tn = woa_tiles.shape
    kern = functools.partial(_merge_kernel, tn=tn, n_tiles=n_tiles)
    row = lambda i, n: (0, 0)
    tiles = pl.BlockSpec((n_tiles, da, tn), lambda i, n: (0, 0, 0), pipeline_mode=pl.Buffered(1))
    return pl.pallas_call(
        kern,
        grid=(s // tm, n_tiles),
        in_specs=[pl.BlockSpec((tm, da), lambda i, n: (i, 0)),
                  pl.BlockSpec((tm, da), lambda i, n: (i, 0)),
                  pl.BlockSpec((tm, tn), lambda i, n: (i, ga_col // tn + n)),
                  pl.BlockSpec((tm, tn), lambda i, n: (i, gb_col // tn + n)),
                  tiles, tiles,
                  pl.BlockSpec((d, d), row, pipeline_mode=pl.Buffered(1)),
                  pl.BlockSpec((tm, d), lambda i, n: (i, 0)),
                  pl.BlockSpec((1, d), row), pl.BlockSpec((1, d), row),
                  pl.BlockSpec((1, d), row), pl.BlockSpec((1, d), row)],
        out_specs=[pl.BlockSpec((tm, d), lambda i, n: (i, 0)),
                   pl.BlockSpec((tm, d), lambda i, n: (i, 0))],
        out_shape=[jax.ShapeDtypeStruct((s, d), jnp.float32),
                   jax.ShapeDtypeStruct((s, d), jnp.bfloat16)],
        scratch_shapes=[pltpu.VMEM((tm, d), jnp.bfloat16)],
        compiler_params=_cparams(("parallel", "arbitrary")),
        name="merge",
    )(ya, yb, proj, proj, woa_tiles, wob_tiles, w_out, x, g1, n2, sc2, sh2)


def _ffn_kernel(h_ref, wg_ref, wu_ref, wd_ref, x1_ref, g2_ref, fw_ref, o_ref, acc_ref, *, n_tiles, final_norm):
    f = pl.program_id(1)

    @pl.when(f == 0)
    def _():
        acc_ref[...] = jnp.zeros(acc_ref.shape, jnp.float32)

    h = h_ref[...]
    g = jnp.dot(h, wg_ref[...], preferred_element_type=jnp.float32)
    u = jnp.dot(h, wu_ref[...], preferred_element_type=jnp.float32)
    a = (g * jax.nn.sigmoid(g) * u).astype(h.dtype)
    acc_ref[...] += jnp.dot(a, wd_ref[...], preferred_element_type=jnp.float32)

    @pl.when(f == n_tiles - 1)
    def _():
        x2 = x1_ref[...] + g2_ref[...] * acc_ref[...]
        if final_norm:
            x2 = x2 * lax.rsqrt(jnp.mean(x2 * x2, axis=-1, keepdims=True) + NORM_EPS) * fw_ref[...]
        o_ref[...] = x2


def _ffn(h2, w_gate, w_up, w_down, x1, g2, final_w, *, final_norm, tm=512, tf=512):
    s, d = x1.shape
    dff = w_gate.shape[1]
    n_tiles = dff // tf
    kern = functools.partial(_ffn_kernel, n_tiles=n_tiles, final_norm=final_norm)
    row = lambda i, f: (0, 0)
    return pl.pallas_call(
        kern,
        grid=(s // tm, n_tiles),
        in_specs=[pl.BlockSpec((tm, d), lambda i, f: (i, 0)),
                  pl.BlockSpec((d, tf), lambda i, f: (0, f)),
                  pl.BlockSpec((d, tf), lambda i, f: (0, f)),
                  pl.BlockSpec((tf, d), lambda i, f: (f, 0)),
                  pl.BlockSpec((tm, d), lambda i, f: (i, 0)),
                  pl.BlockSpec((1, d), row), pl.BlockSpec((1, d), row)],
        out_specs=pl.BlockSpec((tm, d), lambda i, f: (i, 0)),
        out_shape=jax.ShapeDtypeStruct((s, d), jnp.float32),
        scratch_shapes=[pltpu.VMEM((tm, d), jnp.float32)],
        compiler_params=_cparams(("parallel", "arbitrary")),
        name="ffn",
    )(h2, w_gate, w_up, w_down, x1, g2, final_w)


def _rope_tables(s):
    n_rows = s // GRID_W
    axis_dim = HEAD_DIM // 2
    inv = ROPE_THETA ** (-jnp.arange(0, axis_dim, 2, dtype=jnp.float32) / axis_dim)
    lane = np.arange(HEAD_DIM)
    inv_lane = inv[(lane // 2) % (axis_dim // 2)]
    by_col = (lane >= axis_dim)[None, :]
    even = (lane % 2 == 0)[None, :]

    def parts(n, mine):
        ang = jnp.arange(n, dtype=jnp.float32)[:, None] * inv_lane[None]
        cos, sin = jnp.cos(ang), jnp.sin(ang)
        tabs = jnp.stack([cos, jnp.where(even, -sin, 0.0), jnp.where(even, 0.0, sin)])
        return jnp.where(mine[None], tabs, 0.0)

    return parts(n_rows, ~by_col), parts(GRID_W, by_col)


def kernel(x, c, w_ada, b_ada, norm1_w, w_in, q_norm_w, k_norm_w, nat_rpb, w_oa, w_ob, w_out, norm2_w,
           w_ffn_gate, w_ffn_up, w_ffn_down, final_w):
    b, s, d = x.shape
    assert b == 1
    depth = w_ada.shape[0]
    n_heads_a = w_oa.shape[1] // HEAD_DIM
    n_heads_b = w_ob.shape[1] // HEAD_DIM
    n_kv_a = n_heads_a // KV_GROUP
    splits = (n_heads_a * HEAD_DIM, n_kv_a * HEAD_DIM, n_kv_a * HEAD_DIM,
              n_heads_b * HEAD_DIM, n_heads_b * HEAD_DIM, n_heads_b * HEAD_DIM, d, d)
    offs = [0] + np.cumsum(splits)[:-1].tolist()
    qa_col, ka_col, va_col, qb_col, kb_col, vb_col, ga_col, gb_col = offs
    assert qa_col == 0

    assert va_col == ka_col + n_kv_a * HEAD_DIM
    rope_rows, rope_cols = _rope_tables(s)
    xs = x[0]
    c_col = c.reshape(d, 1)
    for l in range(depth):
        mod = _adaln(c_col, w_ada[l], b_ada[l][None, :])
        sh1, sc1, g1, sh2, sc2, g2 = [mod[:, i * d:(i + 1) * d] for i in range(6)]
        proj, vt = _inproj(xs, norm1_w[l][None, :], sc1, sh1, w_in[l],
                           q_norm_w[l][None, :], k_norm_w[l][None, :], rope_rows, rope_cols,
                           n_q_heads=n_heads_a, n_k_heads=n_kv_a, qb_cols=(qb_col, kb_col),
                           q_scale=HEAD_DIM ** -0.5 * LOG2E)
        ya = _gqa(proj, vt, n_q_heads=n_heads_a, n_kv_heads=n_kv_a, k_col=ka_col)
        yb, (w_gate, w_up, w_down, w_o, woa_tiles, wob_tiles) = _natten(
            proj, nat_rpb[l],
            [(w_ffn_gate[l], None), (w_ffn_up[l], None), (w_ffn_down[l], None), (w_out[l], None),
             (w_oa[l], MERGE_TILE), (w_ob[l], MERGE_TILE)],
            n_heads=n_heads_b, q_col=qb_col, k_col=kb_col, v_col=vb_col)
        xs, h2 = _merge(ya, yb, proj, woa_tiles, wob_tiles, w_o, xs, g1, norm2_w[l][None, :], sc2, sh2,
                        ga_col=ga_col, gb_col=gb_col)
        xs = _ffn(h2, w_gate, w_up, w_down, xs, g2, final_w[None, :], final_norm=(l == depth - 1))
    return xs[None]
```
